```python
import jax, jax.numpy as jnp
from jax import lax
import numpy as np

D_MODEL = 1024
BATCH = 8
SEQ = 2048
DEPTH = 2
DEC_BATCH = 128
DEC_SEQ = 1
PAST_LEN = 16384
PAGE_SIZE = 128

N_MIXERS = 2
N_HGRN = (DEPTH + 1) // 2
N_POOL = DEPTH // 2
HGRN_EXPAND = 128
HGRN_HEADS = D_MODEL // HGRN_EXPAND
HGRN_DK = HGRN_EXPAND
HGRN_DV = D_MODEL // HGRN_HEADS
HGRN_KTOT = HGRN_HEADS * HGRN_DK
HGRN_VTOT = HGRN_HEADS * HGRN_DV
HGRN_CHUNK = 32
POOL_WINDOWS = (2, 4, 8, 16)
POOL_GROUPS = len(POOL_WINDOWS)
POOL_GW = D_MODEL // POOL_GROUPS
POOL_HIST = max(POOL_WINDOWS) - 1
D_FF = 4 * D_MODEL
EPS = 1e-6

kernel_name = "hgrn2_pool_interleaved_decode_step"


def _rmsnorm(x, g):
    xf = x.astype(jnp.float32)
    y = xf * lax.rsqrt(jnp.mean(xf * xf, axis=-1, keepdims=True) + EPS) * g.astype(jnp.float32)
    return y.astype(x.dtype)


def _gated_recurrence(q, k, v, g, s0):
    B, L, H, _ = q.shape
    C = min(HGRN_CHUNK, L)
    n = -(-L // C)
    Lp = n * C

    def chunks(a):
        a = jnp.pad(a, ((0, 0), (0, Lp - L), (0, 0), (0, 0)))
        return a.reshape(B, n, C, H, a.shape[-1]).swapaxes(0, 1)

    causal = jnp.tril(jnp.ones((C, C), dtype=bool))[None, :, :, None, None]

    def step(S, blk):
        qc, kc, vc, gc = blk
        G = jnp.cumsum(gc, axis=1)
        diff = G[:, :, None] - G[:, None, :]
        decay = jnp.exp(jnp.where(causal, diff, -jnp.inf))
        A = jnp.einsum('bthd,bshd,btshd->btsh', qc, kc, decay)
        o = (jnp.einsum('btsh,bshv->bthv', A, vc)
             + jnp.einsum('bthd,bhdv->bthv', qc * jnp.exp(G), S))
        G_last = G[:, -1]
        S = (jnp.exp(G_last)[..., None] * S
             + jnp.einsum('bshd,bshv->bhdv', kc * jnp.exp(G_last[:, None] - G), vc))
        return S, o

    S, o = lax.scan(step, s0, (chunks(q), chunks(k), chunks(v), chunks(g)))
    o = o.swapaxes(0, 1).reshape(B, Lp, H, v.shape[-1])[:, :L]
    return o, S


def _hgrn2(x, s0, norm_g, w_in, lb, onorm_g, w_out):
    B, L, _ = x.shape
    u = _rmsnorm(x, norm_g)
    proj = jnp.einsum('bld,de->ble', u, w_in).astype(jnp.float32)
    q, fz, inp, og = jnp.split(proj, [HGRN_KTOT, 2 * HGRN_KTOT, 2 * HGRN_KTOT + HGRN_VTOT], axis=-1)
    q = jax.nn.silu(q) * (HGRN_DK ** -0.5)
    f = lb + (1.0 - lb) * jax.nn.sigmoid(fz)
    k = 1.0 - f
    logf = jnp.log(f)
    hd = lambda a, d: a.reshape(B, L, HGRN_HEADS, d)
    o, s_new = _gated_recurrence(hd(q, HGRN_DK), hd(k, HGRN_DK), hd(inp, HGRN_DV),
                                 hd(logf, HGRN_DK), s0.astype(jnp.float32))
    o = o * lax.rsqrt(jnp.mean(o * o, axis=-1, keepdims=True) + EPS)
    o = o.reshape(B, L, HGRN_VTOT) * onorm_g.astype(jnp.float32) * jax.nn.sigmoid(og)
    y = jnp.einsum('ble,ed->bld', o.astype(x.dtype), w_out)
    return y, s_new.astype(s0.dtype)


def _pool(x, hist, start, norm_g, w_grp, scale):
    B, L, _ = x.shape
    u = _rmsnorm(x, norm_g)
    ext = jnp.concatenate([hist.astype(jnp.float32), u.astype(jnp.float32)], axis=1)
    cs = jnp.concatenate([jnp.zeros((B, 1, D_MODEL), jnp.float32), jnp.cumsum(ext, axis=1)], axis=1)
    P = POOL_HIST
    pos = start + jnp.arange(L)
    parts = []
    for gi, w in enumerate(POOL_WINDOWS):
        sl = slice(gi * POOL_GW, (gi + 1) * POOL_GW)
        hi = cs[:, P + 1:P + 1 + L, sl]
        lo = cs[:, P + 1 - w:P + 1 - w + L, sl]
        cnt = jnp.minimum(w, pos + 1).astype(jnp.float32)[None, :, None]
        parts.append((hi - lo) / cnt)
    pooled = jnp.concatenate(parts, axis=-1)
    z = (pooled - u.astype(jnp.float32)).reshape(B, L, POOL_GROUPS, POOL_GW)
    y = jnp.einsum('blgc,gcd->blgd', z.astype(x.dtype), w_grp).reshape(B, L, D_MODEL) * scale
    return y.astype(x.dtype), ext[:, -P:].astype(hist.dtype)


def _mlp(x, norm_g, w_up, w_down):
    h = _rmsnorm(x, norm_g)
    a = jnp.square(jax.nn.relu(jnp.einsum('bld,df->blf', h, w_up)))
    return jnp.einsum('blf,fd->bld', a, w_down)


def _trunk(x, s_hgrn, hist_pool, start, lbs, params):
    (hgrn_norm, hgrn_w_in, hgrn_onorm, hgrn_w_out, pool_norm, pool_w, pool_scale,
     mlp_norm, mlp_up, mlp_down, final_norm) = params
    new_h, new_p = [], []
    for layer in range(DEPTH):
        j = layer // N_MIXERS
        if layer % N_MIXERS == 0:
            y, s = _hgrn2(x, s_hgrn[j], hgrn_norm[j], hgrn_w_in[j], lbs[j], hgrn_onorm[j], hgrn_w_out[j])
            new_h.append(s)
        else:
            y, h = _pool(x, hist_pool[j], start, pool_norm[j], pool_w[j], pool_scale[j])
            new_p.append(h)
        x = x + y
        x = x + _mlp(x, mlp_norm[layer], mlp_up[layer], mlp_down[layer])
    return _rmsnorm(x, final_norm), jnp.stack(new_h), jnp.stack(new_p)


def setup_inputs(seed: int = 0) -> dict:
    key = jax.random.key(seed)
    ks = jax.random.split(key, 20)
    nrm = lambda k, shape, s: jax.random.normal(k, shape, jnp.float32) * s
    gain = lambda k, shape: 1.0 + nrm(k, shape, 0.05)
    return {
        "x_prompt": nrm(ks[0], (BATCH, SEQ, D_MODEL), 1.0),
        "x_sample": nrm(ks[1], (DEC_BATCH, DEC_SEQ, D_MODEL), 1.0),
        "state_hgrn": nrm(ks[2], (N_HGRN, DEC_BATCH, HGRN_HEADS, HGRN_DK, HGRN_DV), 0.5),
        "state_pool": nrm(ks[3], (N_POOL, DEC_BATCH, POOL_HIST, D_MODEL), 1.0),
        "hgrn_norm": gain(ks[4], (N_HGRN, D_MODEL)),
        "hgrn_w_in": nrm(ks[5], (N_HGRN, D_MODEL, 2 * HGRN_KTOT + 2 * HGRN_VTOT), D_MODEL ** -0.5),
        "hgrn_lb": nrm(ks[6], (N_HGRN + 1, HGRN_KTOT), 0.1),
        "hgrn_onorm": gain(ks[7], (N_HGRN, HGRN_VTOT)),
        "hgrn_w_out": nrm(ks[8], (N_HGRN, HGRN_VTOT, D_MODEL), HGRN_VTOT ** -0.5),
        "pool_norm": gain(ks[9], (N_POOL, D_MODEL)),
        "pool_w": nrm(ks[10], (N_POOL, POOL_GROUPS, POOL_GW, POOL_GW), POOL_GW ** -0.5),
        "pool_scale": gain(ks[11], (N_POOL, D_MODEL)),
        "mlp_norm": gain(ks[12], (DEPTH, D_MODEL)),
        "mlp_up": nrm(ks[13], (DEPTH, D_MODEL, D_FF), D_MODEL ** -0.5),
        "mlp_down": nrm(ks[14], (DEPTH, D_FF, D_MODEL), D_FF ** -0.5),
        "final_norm": gain(ks[15], (D_MODEL,)),
    }


def reference(x_prompt, x_sample, state_hgrn, state_pool, hgrn_norm, hgrn_w_in, hgrn_lb, hgrn_onorm,
              hgrn_w_out, pool_norm, pool_w, pool_scale, mlp_norm, mlp_up, mlp_down, final_norm):
    lbs = jnp.cumsum(jax.nn.softmax(hgrn_lb.astype(jnp.float32), axis=0), axis=0)[:N_HGRN]
    params = (hgrn_norm, hgrn_w_in, hgrn_onorm, hgrn_w_out, pool_norm, pool_w, pool_scale,
              mlp_norm, mlp_up, mlp_down, final_norm)
    b = x_prompt.shape[0]
    s0 = jnp.zeros((N_HGRN, b, HGRN_HEADS, HGRN_DK, HGRN_DV), state_hgrn.dtype)
    h0 = jnp.zeros((N_POOL, b, POOL_HIST, D_MODEL), state_pool.dtype)
    y_prompt, hgrn_p, pool_p = _trunk(x_prompt, s0, h0, 0, lbs, params)
    y_sample, hgrn_s, pool_s = _trunk(x_sample, state_hgrn, state_pool, PAST_LEN, lbs, params)
    return (y_prompt, y_sample, hgrn_p, hgrn_s, pool_p, pool_s)
```

```python
import functools

import jax
import jax.numpy as jnp
from jax import lax
from jax.experimental import pallas as pl
from jax.experimental.pallas import tpu as pltpu

D_MODEL = 1024
HEADS = 8
HEAD_DIM = 128
D_FF = 4 * D_MODEL
POOL_WINDOWS = (2, 4, 8, 16)
POOL_GW = D_MODEL // len(POOL_WINDOWS)
POOL_HIST = max(POOL_WINDOWS) - 1
HALO = 16
EPS = 1e-6
PAST_LEN = 16384

CHUNK = 128
FF_CHUNK = 1024
VMEM_LIMIT = 56 * 1024 * 1024

BF16 = jnp.bfloat16
F32 = jnp.float32


def _const_spec(shape):
    nd = len(shape)
    return pl.BlockSpec(shape, lambda *_: (0,) * nd, pipeline_mode=pl.Buffered(1))


def _rms(x, gain):
    return x * lax.rsqrt(jnp.mean(x * x, axis=-1, keepdims=True) + EPS) * gain


def _sigmoid(x):
    return 1.0 / (1.0 + jnp.exp(-x))


def _dot(a, b):
    return jnp.dot(a, b, preferred_element_type=F32)


def _inproj_kernel(x_ref, gain_ref, lb_ref, w_ref, q_ref, k_ref, lg_ref, v_ref, og_ref):
    u = _rms(x_ref[...], gain_ref[...]).astype(BF16)
    lbr = lb_ref[...]
    e = jnp.exp(lbr - jnp.max(lbr, axis=0, keepdims=True))
    lb = e[0:1, :] / jnp.sum(e, axis=0, keepdims=True)

    pq = _dot(u, w_ref[:, 0:D_MODEL])
    q_ref[...] = pq * _sigmoid(pq) * (HEAD_DIM ** -0.5)
    pf = _dot(u, w_ref[:, D_MODEL:2 * D_MODEL])
    f = lb + (1.0 - lb) * _sigmoid(pf)
    k_ref[...] = 1.0 - f
    lg_ref[...] = jnp.log(f)
    v_ref[...] = _dot(u, w_ref[:, 2 * D_MODEL:3 * D_MODEL])
    og_ref[...] = _sigmoid(_dot(u, w_ref[:, 3 * D_MODEL:4 * D_MODEL]))


def _inproj(x, gain, lb, w_in, tm):
    m = x.shape[0]
    row = pl.BlockSpec((tm, D_MODEL), lambda i: (i, 0))
    out = jax.ShapeDtypeStruct((m, D_MODEL), F32)
    return pl.pallas_call(
        _inproj_kernel,
        grid=(m // tm,),
        in_specs=[row, _const_spec((1, D_MODEL)), _const_spec(lb.shape), _const_spec(w_in.shape)],
        out_specs=[row] * 5,
        out_shape=[out] * 5,
        compiler_params=pltpu.CompilerParams(dimension_semantics=("parallel",), vmem_limit_bytes=VMEM_LIMIT),
        name="hgrn_inproj",
    )(x, gain, lb, w_in)


def _cumsum_rows(g):
    n = g.shape[0]
    row = lax.broadcasted_iota(jnp.int32, g.shape, 0)
    s = 1
    while s < n:
        g = g + jnp.where(row >= s, pltpu.roll(g, s, axis=0), 0.0)
        s *= 2
    return g


def _rec_prompt_kernel(q_ref, k_ref, lg_ref, v_ref, o_ref, s_out_ref, st_ref, *, n_chunks):
    t = pl.program_id(1)

    @pl.when(t == 0)
    def _():
        st_ref[...] = jnp.zeros_like(st_ref)

    tri = (lax.broadcasted_iota(jnp.int32, (CHUNK, CHUNK), 0)
           >= lax.broadcasted_iota(jnp.int32, (CHUNK, CHUNK), 1))

    def chunk(c, carry):
        rows = pl.ds(pl.multiple_of(c * CHUNK, CHUNK), CHUNK)
        for h in range(HEADS):
            cols = slice(h * HEAD_DIM, (h + 1) * HEAD_DIM)
            big_g = _cumsum_rows(lg_ref[0, rows, cols])
            decay = jnp.exp(big_g)
            qd = (q_ref[0, rows, cols] * decay).astype(BF16)
            kd = (k_ref[0, rows, cols] * jnp.exp(-big_g)).astype(BF16)
            v = v_ref[0, rows, cols].astype(BF16)
            a = lax.dot_general(qd, kd, (((1,), (1,)), ((), ())), preferred_element_type=F32)
            a = jnp.where(tri, a, 0.0).astype(BF16)
            st = st_ref[h]
            o = _dot(a, v) + lax.dot_general(qd, st.astype(BF16), (((1,), (1,)), ((), ())),
                                             preferred_element_type=F32)
            o_ref[0, rows, cols] = o
            ds = lax.dot_general(v, kd, (((0,), (0,)), ((), ())), preferred_element_type=F32)
            st_ref[h] = (st + ds) * decay[CHUNK - 1:CHUNK, :]
        return carry

    lax.fori_loop(0, n_chunks, chunk, 0)

    @pl.when(t == pl.num_programs(1) - 1)
    def _():
        for h in range(HEADS):
            s_out_ref[0, h] = st_ref[h].T


def _rec_prompt(q, k, lg, v, tl):
    b, l, _ = q.shape
    seq = pl.BlockSpec((1, tl, D_MODEL), lambda i, j: (i, j, 0))
    return pl.pallas_call(
        functools.partial(_rec_prompt_kernel, n_chunks=tl // CHUNK),
        grid=(b, l // tl),
        in_specs=[seq] * 4,
        out_specs=[seq, pl.BlockSpec((1, HEADS, HEAD_DIM, HEAD_DIM), lambda i, j: (i, 0, 0, 0))],
        out_shape=[jax.ShapeDtypeStruct((b, l, D_MODEL), F32),
                   jax.ShapeDtypeStruct((b, HEADS, HEAD_DIM, HEAD_DIM), F32)],
        scratch_shapes=[pltpu.VMEM((HEADS, HEAD_DIM, HEAD_DIM), F32)],
        compiler_params=pltpu.CompilerParams(dimension_semantics=("parallel", "arbitrary"),
                                             vmem_limit_bytes=VMEM_LIMIT),
        name="hgrn_recurrence_prompt",
    )(q, k, lg, v)


def _rec_decode_kernel(q_ref, k_ref, v_ref, s_ref, o_ref, s_out_ref):
    n = q_ref.shape[0]
    qt = q_ref[...].T
    kt = k_ref[...].T
    for b in range(n):
        kcol = kt[:, b:b + 1]
        s_new = (1.0 - kcol) * s_ref[b, 0] + kcol * v_ref[b:b + 1, :]
        s_out_ref[b, 0] = s_new
        o_ref[b:b + 1, :] = jnp.sum(qt[:, b:b + 1] * s_new, axis=0, keepdims=True)


def _rec_decode(q, k, v, s0):
    n = q.shape[0]
    head = pl.BlockSpec((n, HEAD_DIM), lambda h: (0, h))
    state = pl.BlockSpec((n, 1, HEAD_DIM, HEAD_DIM), lambda h: (0, h, 0, 0))
    return pl.pallas_call(
        _rec_decode_kernel,
        grid=(HEADS,),
        in_specs=[head, head, head, state],
        out_specs=[head, state],
        out_shape=[jax.ShapeDtypeStruct((n, D_MODEL), F32), jax.ShapeDtypeStruct(s0.shape, F32)],
        compiler_params=pltpu.CompilerParams(dimension_semantics=("parallel",), vmem_limit_bytes=VMEM_LIMIT),
        name="hgrn_recurrence_decode",
    )(q, k, v, s0)


def _mlp_residual(x, gain_ref, wup_ref, wdn_ref):
    hn = _rms(x, gain_ref[...]).astype(BF16)
    acc = x
    for c in range(D_FF // FF_CHUNK):
        cols = slice(c * FF_CHUNK, (c + 1) * FF_CHUNK)
        a = jnp.square(jnp.maximum(_dot(hn, wup_ref[:, cols]), 0.0)).astype(BF16)
        acc = acc + _dot(a, wdn_ref[cols, :])
    return acc


def _hgrn_out_mlp_kernel(o_ref, og_ref, x_ref, onorm_ref, wout_ref, gain_ref, wup_ref, wdn_ref, out_ref):
    heads = []
    for h in range(HEADS):
        oh = o_ref[:, h * HEAD_DIM:(h + 1) * HEAD_DIM]
        heads.append(oh * lax.rsqrt(jnp.mean(oh * oh, axis=-1, keepdims=True) + EPS))
    gated = (jnp.concatenate(heads, axis=-1) * onorm_ref[...] * og_ref[...]).astype(BF16)
    x1 = x_ref[...] + _dot(gated, wout_ref[...])
    out_ref[...] = _mlp_residual(x1, gain_ref, wup_ref, wdn_ref)


def _hgrn_out_mlp(o, og, x, onorm, w_out, gain, w_up, w_dn, tm):
    m = x.shape[0]
    row = pl.BlockSpec((tm, D_MODEL), lambda i: (i, 0))
    vec = _const_spec((1, D_MODEL))
    return pl.pallas_call(
        _hgrn_out_mlp_kernel,
        grid=(m // tm,),
        in_specs=[row, row, row, vec, _const_spec(w_out.shape), vec, _const_spec(w_up.shape),
                  _const_spec(w_dn.shape)],
        out_specs=row,
        out_shape=jax.ShapeDtypeStruct((m, D_MODEL), F32),
        compiler_params=pltpu.CompilerParams(dimension_semantics=("parallel",), vmem_limit_bytes=VMEM_LIMIT),
        name="hgrn_out_mlp",
    )(o, og, x, onorm, w_out, gain, w_up, w_dn)


def _pool_project(z, wgrp_ref, scale_ref):
    parts = [_dot(z[:, g * POOL_GW:(g + 1) * POOL_GW].astype(BF16), wgrp_ref[g]) for g in range(len(POOL_WINDOWS))]
    return jnp.concatenate(parts, axis=-1) * scale_ref[...]


def _pool_prompt_kernel(x_ref, halo_ref, pgain_ref, wgrp_ref, scale_ref, gain_ref, wup_ref, wdn_ref, fgain_ref,
                        y_ref, hist_ref, ext_ref, *, tm):
    t = pl.program_id(1)
    x = x_ref[0]
    u = _rms(x, pgain_ref[...])
    ext_ref[0:HALO, :] = jnp.where(t > 0, _rms(halo_ref[0], pgain_ref[...]), 0.0)
    ext_ref[HALO:, :] = u
    pos = t * tm + lax.broadcasted_iota(jnp.int32, (tm, 1), 0)
    parts = []
    for g, w in enumerate(POOL_WINDOWS):
        cols = slice(g * POOL_GW, (g + 1) * POOL_GW)
        acc = u[:, cols]
        for j in range(1, w):
            acc = acc + ext_ref[HALO - j:HALO - j + tm, cols]
        cnt = jnp.minimum(w, pos + 1).astype(F32)
        parts.append(acc / cnt - u[:, cols])
    z = jnp.concatenate(parts, axis=-1)
    x1 = x + _pool_project(z, wgrp_ref, scale_ref)
    y_ref[0] = _rms(_mlp_residual(x1, gain_ref, wup_ref, wdn_ref), fgain_ref[...])

    @pl.when(t == pl.num_programs(1) - 1)
    def _():
        hist_ref[0] = ext_ref[HALO + tm - POOL_HIST:HALO + tm, :]


def _pool_prompt(x, pgain, wgrp, scale, gain, w_up, w_dn, fgain, tm):
    b, l, _ = x.shape
    vec = _const_spec((1, D_MODEL))
    blocks_per_tile = tm // HALO
    return pl.pallas_call(
        functools.partial(_pool_prompt_kernel, tm=tm),
        grid=(b, l // tm),
        in_specs=[pl.BlockSpec((1, tm, D_MODEL), lambda i, j: (i, j, 0)),
                  pl.BlockSpec((1, HALO, D_MODEL), lambda i, j: (i, jnp.maximum(j * blocks_per_tile - 1, 0), 0)),
                  vec, _const_spec(wgrp.shape), vec, vec, _const_spec(w_up.shape), _const_spec(w_dn.shape), vec],
        out_specs=[pl.BlockSpec((1, tm, D_MODEL), lambda i, j: (i, j, 0)),
                   pl.BlockSpec((1, POOL_HIST, D_MODEL), lambda i, j: (i, 0, 0))],
        out_shape=[jax.ShapeDtypeStruct((b, l, D_MODEL), F32), jax.ShapeDtypeStruct((b, POOL_HIST, D_MODEL), F32)],
        scratch_shapes=[pltpu.VMEM((HALO + tm, D_MODEL), F32)],
        compiler_params=pltpu.CompilerParams(dimension_semantics=("parallel", "arbitrary"),
                                             vmem_limit_bytes=VMEM_LIMIT),
        name="pool_mlp_prompt",
    )(x, x, pgain, wgrp, scale, gain, w_up, w_dn, fgain)


def _pool_decode_kernel(x_ref, hist_ref, pgain_ref, wgrp_ref, scale_ref, gain_ref, wup_ref, wdn_ref, fgain_ref,
                        y_ref, hist_out_ref):
    x = x_ref[...]
    u = _rms(x, pgain_ref[...])
    parts = []
    for g, w in enumerate(POOL_WINDOWS):
        acc = u[:, g * POOL_GW:(g + 1) * POOL_GW]
        for j in range(1, w):
            lo = (POOL_HIST - j) * D_MODEL + g * POOL_GW
            acc = acc + hist_ref[:, lo:lo + POOL_GW]
        parts.append(acc / float(min(w, PAST_LEN + 1)) - u[:, g * POOL_GW:(g + 1) * POOL_GW])
    z = jnp.concatenate(parts, axis=-1)
    x1 = x + _pool_project(z, wgrp_ref, scale_ref)
    y_ref[...] = _rms(_mlp_residual(x1, gain_ref, wup_ref, wdn_ref), fgain_ref[...])
    hist_out_ref[:, 0:(POOL_HIST - 1) * D_MODEL] = hist_ref[:, D_MODEL:]
    hist_out_ref[:, (POOL_HIST - 1) * D_MODEL:] = u


def _pool_decode(x, hist, pgain, wgrp, scale, gain, w_up, w_dn, fgain):
    n = x.shape[0]
    vec = _const_spec((1, D_MODEL))
    return pl.pallas_call(
        _pool_decode_kernel,
        grid=(1,),
        in_specs=[_const_spec(x.shape), _const_spec(hist.shape), vec, _const_spec(wgrp.shape), vec, vec,
                  _const_spec(w_up.shape), _const_spec(w_dn.shape), vec],
        out_specs=[pl.BlockSpec(x.shape, lambda i: (0, 0)), pl.BlockSpec(hist.shape, lambda i: (0, 0))],
        out_shape=[jax.ShapeDtypeStruct(x.shape, F32), jax.ShapeDtypeStruct(hist.shape, F32)],
        compiler_params=pltpu.CompilerParams(dimension_semantics=("arbitrary",), vmem_limit_bytes=VMEM_LIMIT),
        name="pool_mlp_decode",
    )(x, hist, pgain, wgrp, scale, gain, w_up, w_dn, fgain)


def kernel(x_prompt, x_sample, state_hgrn, state_pool, hgrn_norm, hgrn_w_in, hgrn_lb, hgrn_onorm, hgrn_w_out,
           pool_norm, pool_w, pool_scale, mlp_norm, mlp_up, mlp_down, final_norm):
    b, l, _ = x_prompt.shape
    n = x_sample.shape[0]
    vec = lambda a: a.reshape(1, D_MODEL)
    w_in = hgrn_w_in[0].astype(BF16)
    w_out = hgrn_w_out[0].astype(BF16)
    w_pool = pool_w[0].astype(BF16)
    w_up = mlp_up.astype(BF16)
    w_dn = mlp_down.astype(BF16)

    def layer0(x, tm):
        q, k, lg, v, og = _inproj(x, vec(hgrn_norm[0]), hgrn_lb, w_in, tm)
        return q, k, lg, v, og

    def layer0_out(o, og, x, tm):
        return _hgrn_out_mlp(o, og, x, vec(hgrn_onorm[0]), w_out, vec(mlp_norm[0]), w_up[0], w_dn[0], tm)

    pool_args = (vec(pool_norm[0]), w_pool, vec(pool_scale[0]), vec(mlp_norm[1]), w_up[1], w_dn[1], vec(final_norm))

    xp = x_prompt.reshape(b * l, D_MODEL)
    q, k, lg, v, og = layer0(xp, 512)
    seq = lambda a: a.reshape(b, l, D_MODEL)
    o, hgrn_p = _rec_prompt(seq(q), seq(k), seq(lg), seq(v), 512)
    x2 = layer0_out(o.reshape(b * l, D_MODEL), og, xp, 512)
    y_prompt, pool_p = _pool_prompt(seq(x2), *pool_args, tm=512)

    xs = x_sample.reshape(n, D_MODEL)
    q, k, lg, v, og = layer0(xs, n)
    o, hgrn_s = _rec_decode(q, k, v, state_hgrn[0])
    x2 = layer0_out(o, og, xs, n)
    y_sample, pool_s = _pool_decode(x2, state_pool[0].reshape(n, POOL_HIST * D_MODEL), *pool_args)

    return (y_prompt, y_sample.reshape(n, 1, D_MODEL), hgrn_p[None], hgrn_s[None], pool_p[None],
            pool_s.reshape(1, n, POOL_HIST, D_MODEL))
```

```python
import functools

import jax
import jax.numpy as jnp
from jax import lax
from jax.experimental import pallas as pl
from jax.experimental.pallas import tpu as pltpu

D_MODEL = 1024
HEADS = 8
HEAD_DIM = 128
D_FF = 4 * D_MODEL
POOL_WINDOWS = (2, 4, 8, 16)
POOL_GW = D_MODEL // len(POOL_WINDOWS)
POOL_HIST = max(POOL_WINDOWS) - 1
HALO = 16
EPS = 1e-6
PAST_LEN = 16384

CHUNK = 128
FF_CHUNK = 1024
VMEM_LIMIT = 56 * 1024 * 1024

BF16 = jnp.bfloat16
F32 = jnp.float32


def _const_spec(shape):
    nd = len(shape)
    return pl.BlockSpec(shape, lambda *_: (0,) * nd, pipeline_mode=pl.Buffered(1))


def _rms(x, gain):
    return x * lax.rsqrt(jnp.mean(x * x, axis=-1, keepdims=True) + EPS) * gain


def _sigmoid(x):
    return 1.0 / (1.0 + jnp.exp(-x))


def _dot(a, b):
    return jnp.dot(a, b, preferred_element_type=F32)


def _inproj_kernel(x_ref, gain_ref, lb_ref, w_ref, q_ref, f_ref, v_ref, og_ref):
    u = _rms(x_ref[...], gain_ref[...]).astype(BF16)
    lbr = lb_ref[...]
    e = jnp.exp(lbr - jnp.max(lbr, axis=0, keepdims=True))
    lb = e[0:1, :] / jnp.sum(e, axis=0, keepdims=True)

    pq = _dot(u, w_ref[:, 0:D_MODEL])
    q_ref[...] = pq * _sigmoid(pq) * (HEAD_DIM ** -0.5)
    pf = _dot(u, w_ref[:, D_MODEL:2 * D_MODEL])
    f_ref[...] = lb + (1.0 - lb) * _sigmoid(pf)
    v_ref[...] = _dot(u, w_ref[:, 2 * D_MODEL:3 * D_MODEL]).astype(BF16)
    og_ref[...] = _sigmoid(_dot(u, w_ref[:, 3 * D_MODEL:4 * D_MODEL]))


def _inproj(x, gain, lb, w_in, tm):
    m = x.shape[0]
    row = pl.BlockSpec((tm, D_MODEL), lambda i: (i, 0))
    out = lambda dt: jax.ShapeDtypeStruct((m, D_MODEL), dt)
    return pl.pallas_call(
        _inproj_kernel,
        grid=(m // tm,),
        in_specs=[row, _const_spec((1, D_MODEL)), _const_spec(lb.shape), _const_spec(w_in.shape)],
        out_specs=[row] * 4,
        out_shape=[out(F32), out(F32), out(BF16), out(F32)],
        compiler_params=pltpu.CompilerParams(dimension_semantics=("parallel",), vmem_limit_bytes=VMEM_LIMIT),
        name="hgrn_inproj",
    )(x, gain, lb, w_in)


def _cumsum_rows(g):
    n = g.shape[0]
    row = lax.broadcasted_iota(jnp.int32, g.shape, 0)
    s = 1
    while s < n:
        g = g + jnp.where(row >= s, pltpu.roll(g, s, axis=0), 0.0)
        s *= 2
    return g


def _rec_prompt_kernel(q_ref, f_ref, v_ref, o_ref, s_out_ref, st_ref, *, n_chunks):
    t = pl.program_id(1)

    @pl.when(t == 0)
    def _():
        st_ref[...] = jnp.zeros_like(st_ref)

    tri = (lax.broadcasted_iota(jnp.int32, (CHUNK, CHUNK), 0)
           >= lax.broadcasted_iota(jnp.int32, (CHUNK, CHUNK), 1))

    def chunk(c, carry):
        rows = pl.ds(pl.multiple_of(c * CHUNK, CHUNK), CHUNK)
        for h in range(HEADS):
            cols = slice(h * HEAD_DIM, (h + 1) * HEAD_DIM)
            f = f_ref[0, rows, cols]
            big_g = _cumsum_rows(jnp.log(f))
            decay = jnp.exp(big_g)
            qd = (q_ref[0, rows, cols] * decay).astype(BF16)
            kd = ((1.0 - f) * jnp.exp(-big_g)).astype(BF16)
            v = v_ref[0, rows, cols]
            a = lax.dot_general(qd, kd, (((1,), (1,)), ((), ())), preferred_element_type=F32)
            a = jnp.where(tri, a, 0.0).astype(BF16)
            st = st_ref[h]
            o = _dot(a, v) + lax.dot_general(qd, st.astype(BF16), (((1,), (1,)), ((), ())),
                                             preferred_element_type=F32)
            o_ref[0, rows, cols] = o
            ds = lax.dot_general(v, kd, (((0,), (0,)), ((), ())), preferred_element_type=F32)
            st_ref[h] = (st + ds) * decay[CHUNK - 1:CHUNK, :]
        return carry

    lax.fori_loop(0, n_chunks, chunk, 0)

    @pl.when(t == pl.num_programs(1) - 1)
    def _():
        for h in range(HEADS):
            s_out_ref[0, h] = st_ref[h].T


def _rec_prompt(q, f, v, tl):
    b, l, _ = q.shape
    seq = pl.BlockSpec((1, tl, D_MODEL), lambda i, j: (i, j, 0))
    return pl.pallas_call(
        functools.partial(_rec_prompt_kernel, n_chunks=tl // CHUNK),
        grid=(b, l // tl),
        in_specs=[seq] * 3,
        out_specs=[seq, pl.BlockSpec((1, HEADS, HEAD_DIM, HEAD_DIM), lambda i, j: (i, 0, 0, 0))],
        out_shape=[jax.ShapeDtypeStruct((b, l, D_MODEL), F32),
                   jax.ShapeDtypeStruct((b, HEADS, HEAD_DIM, HEAD_DIM), F32)],
        scratch_shapes=[pltpu.VMEM((HEADS, HEAD_DIM, HEAD_DIM), F32)],
        compiler_params=pltpu.CompilerParams(dimension_semantics=("parallel", "arbitrary"),
                                             vmem_limit_bytes=VMEM_LIMIT),
        name="hgrn_recurrence_prompt",
    )(q, f, v)


def _rec_decode_kernel(q_ref, f_ref, v_ref, s_ref, o_ref, s_out_ref):
    n = q_ref.shape[0]
    qt = q_ref[...].T
    ft = f_ref[...].T
    v = v_ref[...].astype(F32)
    for b in range(n):
        fcol = ft[:, b:b + 1]
        s_new = fcol * s_ref[b, 0] + (1.0 - fcol) * v[b:b + 1, :]
        s_out_ref[b, 0] = s_new
        o_ref[b:b + 1, :] = jnp.sum(qt[:, b:b + 1] * s_new, axis=0, keepdims=True)


def _rec_decode(q, f, v, s0):
    n = q.shape[0]
    head = pl.BlockSpec((n, HEAD_DIM), lambda h: (0, h))
    state = pl.BlockSpec((n, 1, HEAD_DIM, HEAD_DIM), lambda h: (0, h, 0, 0))
    return pl.pallas_call(
        _rec_decode_kernel,
        grid=(HEADS,),
        in_specs=[head, head, head, state],
        out_specs=[head, state],
        out_shape=[jax.ShapeDtypeStruct((n, D_MODEL), F32), jax.ShapeDtypeStruct(s0.shape, F32)],
        compiler_params=pltpu.CompilerParams(dimension_semantics=("parallel",), vmem_limit_bytes=VMEM_LIMIT),
        name="hgrn_recurrence_decode",
    )(q, f, v, s0)


def _mlp_residual(x, gain_ref, wup_ref, wdn_ref):
    hn = _rms(x, gain_ref[...]).astype(BF16)
    acc = x
    for c in range(D_FF // FF_CHUNK):
        cols = slice(c * FF_CHUNK, (c + 1) * FF_CHUNK)
        a = jnp.square(jnp.maximum(_dot(hn, wup_ref[:, cols]), 0.0)).astype(BF16)
        acc = acc + _dot(a, wdn_ref[cols, :])
    return acc


def _hgrn_out_mlp_kernel(o_ref, og_ref, x_ref, onorm_ref, wout_ref, gain_ref, wup_ref, wdn_ref, out_ref):
    heads = []
    for h in range(HEADS):
        oh = o_ref[:, h * HEAD_DIM:(h + 1) * HEAD_DIM]
        heads.append(oh * lax.rsqrt(jnp.mean(oh * oh, axis=-1, keepdims=True) + EPS))
    gated = (jnp.concatenate(heads, axis=-1) * onorm_ref[...] * og_ref[...]).astype(BF16)
    x1 = x_ref[...] + _dot(gated, wout_ref[...])
    out_ref[...] = _mlp_residual(x1, gain_ref, wup_ref, wdn_ref)


def _hgrn_out_mlp(o, og, x, onorm, w_out, gain, w_up, w_dn, tm):
    m = x.shape[0]
    row = pl.BlockSpec((tm, D_MODEL), lambda i: (i, 0))
    vec = _const_spec((1, D_MODEL))
    return pl.pallas_call(
        _hgrn_out_mlp_kernel,
        grid=(m // tm,),
        in_specs=[row, row, row, vec, _const_spec(w_out.shape), vec, _const_spec(w_up.shape),
                  _const_spec(w_dn.shape)],
        out_specs=row,
        out_shape=jax.ShapeDtypeStruct((m, D_MODEL), F32),
        compiler_params=pltpu.CompilerParams(dimension_semantics=("parallel",), vmem_limit_bytes=VMEM_LIMIT),
        name="hgrn_out_mlp",
    )(o, og, x, onorm, w_out, gain, w_up, w_dn)


def _pool_project(z, wgrp_ref, scale_ref):
    parts = [_dot(z[:, g * POOL_GW:(g + 1) * POOL_GW].astype(BF16), wgrp_ref[g]) for g in range(len(POOL_WINDOWS))]
    return jnp.concatenate(parts, axis=-1) * scale_ref[...]


def _pool_prompt_kernel(x_ref, halo_ref, pgain_ref, wgrp_ref, scale_ref, gain_ref, wup_ref, wdn_ref, fgain_ref,
                        y_ref, hist_ref, ext_ref, *, tm):
    t = pl.program_id(1)
    x = x_ref[0]
    u = _rms(x, pgain_ref[...])
    ext_ref[0:HALO, :] = jnp.where(t > 0, _rms(halo_ref[0], pgain_ref[...]), 0.0)
    ext_ref[HALO:, :] = u
    pos = t * tm + lax.broadcasted_iota(jnp.int32, (tm, 1), 0)
    parts = []
    for g, w in enumerate(POOL_WINDOWS):
        cols = slice(g * POOL_GW, (g + 1) * POOL_GW)
        acc = u[:, cols]
        for j in range(1, w):
            acc = acc + ext_ref[HALO - j:HALO - j + tm, cols]
        cnt = jnp.minimum(w, pos + 1).astype(F32)
        parts.append(acc / cnt - u[:, cols])
    z = jnp.concatenate(parts, axis=-1)
    x1 = x + _pool_project(z, wgrp_ref, scale_ref)
    y_ref[0] = _rms(_mlp_residual(x1, gain_ref, wup_ref, wdn_ref), fgain_ref[...])

    @pl.when(t == pl.num_programs(1) - 1)
    def _():
        hist_ref[0] = ext_ref[HALO + tm - POOL_HIST:HALO + tm, :]


def _pool_prompt(x, pgain, wgrp, scale, gain, w_up, w_dn, fgain, tm):
    b, l, _ = x.shape
    vec = _const_spec((1, D_MODEL))
    blocks_per_tile = tm // HALO
    return pl.pallas_call(
        functools.partial(_pool_prompt_kernel, tm=tm),
        grid=(b, l // tm),
        in_specs=[pl.BlockSpec((1, tm, D_MODEL), lambda i, j: (i, j, 0)),
                  pl.BlockSpec((1, HALO, D_MODEL), lambda i, j: (i, jnp.maximum(j * blocks_per_tile - 1, 0), 0)),
                  vec, _const_spec(wgrp.shape), vec, vec, _const_spec(w_up.shape), _const_spec(w_dn.shape), vec],
        out_specs=[pl.BlockSpec((1, tm, D_MODEL), lambda i, j: (i, j, 0)),
                   pl.BlockSpec((1, POOL_HIST, D_MODEL), lambda i, j: (i, 0, 0))],
        out_shape=[jax.ShapeDtypeStruct((b, l, D_MODEL), F32), jax.ShapeDtypeStruct((b, POOL_HIST, D_MODEL), F32)],
        scratch_shapes=[pltpu.VMEM((HALO + tm, D_MODEL), F32)],
        compiler_params=pltpu.CompilerParams(dimension_semantics=("parallel", "arbitrary"),
                                             vmem_limit_bytes=VMEM_LIMIT),
        name="pool_mlp_prompt",
    )(x, x, pgain, wgrp, scale, gain, w_up, w_dn, fgain)


def _pool_decode_kernel(x_ref, hist_ref, pgain_ref, wgrp_ref, scale_ref, gain_ref, wup_ref, wdn_ref, fgain_ref,
                        y_ref, hist_out_ref):
    x = x_ref[...]
    u = _rms(x, pgain_ref[...])
    parts = []
    for g, w in enumerate(POOL_WINDOWS):
        cols = slice(g * POOL_GW, (g + 1) * POOL_GW)
        acc = u[:, cols]
        for j in range(1, w):
            acc = acc + hist_ref[POOL_HIST - j, :, cols]
        parts.append(acc / float(min(w, PAST_LEN + 1)) - u[:, cols])
    z = jnp.concatenate(parts, axis=-1)
    x1 = x + _pool_project(z, wgrp_ref, scale_ref)
    y_ref[...] = _rms(_mlp_residual(x1, gain_ref, wup_ref, wdn_ref), fgain_ref[...])
    hist_out_ref[0:POOL_HIST - 1] = hist_ref[1:POOL_HIST]
    hist_out_ref[POOL_HIST - 1] = u


def _pool_decode(x, hist, pgain, wgrp, scale, gain, w_up, w_dn, fgain):
    n = x.shape[0]
    vec = _const_spec((1, D_MODEL))
    return pl.pallas_call(
        _pool_decode_kernel,
        grid=(1,),
        in_specs=[_const_spec(x.shape), _const_spec(hist.shape), vec, _const_spec(wgrp.shape), vec, vec,
                  _const_spec(w_up.shape), _const_spec(w_dn.shape), vec],
        out_specs=[pl.BlockSpec(x.shape, lambda i: (0, 0)), pl.BlockSpec(hist.shape, lambda i: (0, 0, 0))],
        out_shape=[jax.ShapeDtypeStruct(x.shape, F32), jax.ShapeDtypeStruct(hist.shape, F32)],
        compiler_params=pltpu.CompilerParams(dimension_semantics=("arbitrary",), vmem_limit_bytes=VMEM_LIMIT),
        name="pool_mlp_decode",
    )(x, hist, pgain, wgrp, scale, gain, w_up, w_dn, fgain)


def kernel(x_prompt, x_sample, state_hgrn, state_pool, hgrn_norm, hgrn_w_in, hgrn_lb, hgrn_onorm, hgrn_w_out,
           pool_norm, pool_w, pool_scale, mlp_norm, mlp_up, mlp_down, final_norm):
    b, l, _ = x_prompt.shape
    n = x_sample.shape[0]
    vec = lambda a: a.reshape(1, D_MODEL)
    w_in = hgrn_w_in[0].astype(BF16)
    w_out = hgrn_w_out[0].astype(BF16)
    w_pool = pool_w[0].astype(BF16)
    w_up = [mlp_up[i].astype(BF16) for i in range(2)]
    w_dn = [mlp_down[i].astype(BF16) for i in range(2)]

    def layer0(x, tm):
        return _inproj(x, vec(hgrn_norm[0]), hgrn_lb, w_in, tm)

    def layer0_out(o, og, x, tm):
        return _hgrn_out_mlp(o, og, x, vec(hgrn_onorm[0]), w_out, vec(mlp_norm[0]), w_up[0], w_dn[0], tm)

    pool_args = (vec(pool_norm[0]), w_pool, vec(pool_scale[0]), vec(mlp_norm[1]), w_up[1], w_dn[1], vec(final_norm))

    xp = x_prompt.reshape(b * l, D_MODEL)
    q, f, v, og = layer0(xp, 512)
    seq = lambda a: a.reshape(b, l, D_MODEL)
    o, hgrn_p = _rec_prompt(seq(q), seq(f), seq(v), 512)
    x2 = layer0_out(o.reshape(b * l, D_MODEL), og, xp, 512)
    y_prompt, pool_p = _pool_prompt(seq(x2), *pool_args, tm=512)

    xs = x_sample.reshape(n, D_MODEL)
    q, f, v, og = layer0(xs, n)
    o, hgrn_s = _rec_decode(q, f, v, state_hgrn[0])
    x2 = layer0_out(o, og, xs, n)
    y_sample, pool_s = _pool_decode(x2, jnp.transpose(state_pool[0], (1, 0, 2)), *pool_args)

    return (y_prompt, y_sample.reshape(n, 1, D_MODEL), hgrn_p[None], hgrn_s[None], pool_p[None],
            jnp.transpose(pool_s, (1, 0, 2))[None])
```

```python
import functools

import jax
import jax.numpy as jnp
from jax import lax
from jax.experimental import pallas as pl
from jax.experimental.pallas import tpu as pltpu

D_MODEL = 1024
HEADS = 8
HEAD_DIM = 128
D_FF = 4 * D_MODEL
POOL_WINDOWS = (2, 4, 8, 16)
POOL_GW = D_MODEL // len(POOL_WINDOWS)
POOL_HIST = max(POOL_WINDOWS) - 1
HALO = 16
EPS = 1e-6
PAST_LEN = 16384

CHUNK = 128
FF_CHUNK = 1024
VMEM_LIMIT = 56 * 1024 * 1024

BF16 = jnp.bfloat16
F32 = jnp.float32


def _const_spec(shape):
    nd = len(shape)
    return pl.BlockSpec(shape, lambda *_: (0,) * nd, pipeline_mode=pl.Buffered(1))


def _layer_spec(shape, layer):
    return pl.BlockSpec((None,) + tuple(shape[1:]), lambda *_: (layer, 0, 0), pipeline_mode=pl.Buffered(1))


def _rms(x, gain):
    return x * lax.rsqrt(jnp.mean(x * x, axis=-1, keepdims=True) + EPS) * gain


def _sigmoid(x):
    return 1.0 / (1.0 + jnp.exp(-x))


def _dot(a, b):
    return jnp.dot(a, b, preferred_element_type=F32)


def _inproj_kernel(x_ref, gain_ref, lb_ref, w_ref, q_ref, f_ref, v_ref, og_ref):
    u = _rms(x_ref[...], gain_ref[...]).astype(BF16)
    lbr = lb_ref[...]
    e = jnp.exp(lbr - jnp.max(lbr, axis=0, keepdims=True))
    lb = e[0:1, :] / jnp.sum(e, axis=0, keepdims=True)

    pq = _dot(u, w_ref[:, 0:D_MODEL])
    q_ref[...] = pq * _sigmoid(pq) * (HEAD_DIM ** -0.5)
    pf = _dot(u, w_ref[:, D_MODEL:2 * D_MODEL])
    f_ref[...] = lb + (1.0 - lb) * _sigmoid(pf)
    v_ref[...] = _dot(u, w_ref[:, 2 * D_MODEL:3 * D_MODEL]).astype(BF16)
    og_ref[...] = _sigmoid(_dot(u, w_ref[:, 3 * D_MODEL:4 * D_MODEL]))


def _inproj(x, gain, lb, w_in, tm):
    m = x.shape[0]
    row = pl.BlockSpec((tm, D_MODEL), lambda i: (i, 0))
    out = lambda dt: jax.ShapeDtypeStruct((m, D_MODEL), dt)
    return pl.pallas_call(
        _inproj_kernel,
        grid=(m // tm,),
        in_specs=[row, _const_spec((1, D_MODEL)), _const_spec(lb.shape), _const_spec(w_in.shape)],
        out_specs=[row] * 4,
        out_shape=[out(F32), out(F32), out(BF16), out(F32)],
        compiler_params=pltpu.CompilerParams(dimension_semantics=("parallel",), vmem_limit_bytes=VMEM_LIMIT),
        name="hgrn_inproj",
    )(x, gain, lb, w_in)


def _cumsum_rows(g):
    n = g.shape[0]
    row = lax.broadcasted_iota(jnp.int32, g.shape, 0)
    s = 1
    while s < n:
        g = g + jnp.where(row >= s, pltpu.roll(g, s, axis=0), 0.0)
        s *= 2
    return g


def _rec_prompt_kernel(q_ref, f_ref, v_ref, o_ref, s_out_ref, st_ref, *, n_chunks):
    t = pl.program_id(1)

    @pl.when(t == 0)
    def _():
        st_ref[...] = jnp.zeros_like(st_ref)

    tri = (lax.broadcasted_iota(jnp.int32, (CHUNK, CHUNK), 0)
           >= lax.broadcasted_iota(jnp.int32, (CHUNK, CHUNK), 1))

    def chunk(c, carry):
        rows = pl.ds(pl.multiple_of(c * CHUNK, CHUNK), CHUNK)
        for h in range(HEADS):
            cols = slice(h * HEAD_DIM, (h + 1) * HEAD_DIM)
            f = f_ref[0, rows, cols]
            big_g = _cumsum_rows(jnp.log(f))
            decay = jnp.exp(big_g)
            qd = (q_ref[0, rows, cols] * decay).astype(BF16)
            kd = ((1.0 - f) * jnp.exp(-big_g)).astype(BF16)
            v = v_ref[0, rows, cols]
            a = lax.dot_general(qd, kd, (((1,), (1,)), ((), ())), preferred_element_type=F32)
            a = jnp.where(tri, a, 0.0).astype(BF16)
            st = st_ref[h]
            o = _dot(a, v) + lax.dot_general(qd, st.astype(BF16), (((1,), (1,)), ((), ())),
                                             preferred_element_type=F32)
            o_ref[0, rows, cols] = o
            ds = lax.dot_general(v, kd, (((0,), (0,)), ((), ())), preferred_element_type=F32)
            st_ref[h] = (st + ds) * decay[CHUNK - 1:CHUNK, :]
        return carry

    lax.fori_loop(0, n_chunks, chunk, 0)

    @pl.when(t == pl.num_programs(1) - 1)
    def _():
        for h in range(HEADS):
            s_out_ref[0, h] = st_ref[h].T


def _rec_prompt(q, f, v, tl):
    b, l, _ = q.shape
    seq = pl.BlockSpec((1, tl, D_MODEL), lambda i, j: (i, j, 0))
    return pl.pallas_call(
        functools.partial(_rec_prompt_kernel, n_chunks=tl // CHUNK),
        grid=(b, l // tl),
        in_specs=[seq] * 3,
        out_specs=[seq, pl.BlockSpec((1, HEADS, HEAD_DIM, HEAD_DIM), lambda i, j: (i, 0, 0, 0))],
        out_shape=[jax.ShapeDtypeStruct((b, l, D_MODEL), F32),
                   jax.ShapeDtypeStruct((b, HEADS, HEAD_DIM, HEAD_DIM), F32)],
        scratch_shapes=[pltpu.VMEM((HEADS, HEAD_DIM, HEAD_DIM), F32)],
        compiler_params=pltpu.CompilerParams(dimension_semantics=("parallel", "arbitrary"),
                                             vmem_limit_bytes=VMEM_LIMIT),
        name="hgrn_recurrence_prompt",
    )(q, f, v)


def _rec_decode_kernel(q_ref, f_ref, v_ref, s_ref, o_ref, s_out_ref):
    n = q_ref.shape[0]
    qt = q_ref[...].T
    ft = f_ref[...].T
    v = v_ref[...].astype(F32)
    for b in range(n):
        fcol = ft[:, b:b + 1]
        s_new = fcol * s_ref[b, 0] + (1.0 - fcol) * v[b:b + 1, :]
        s_out_ref[b, 0] = s_new
        o_ref[b:b + 1, :] = jnp.sum(qt[:, b:b + 1] * s_new, axis=0, keepdims=True)


def _rec_decode(q, f, v, s0):
    n = q.shape[0]
    head = pl.BlockSpec((n, HEAD_DIM), lambda h: (0, h))
    state = pl.BlockSpec((n, 1, HEAD_DIM, HEAD_DIM), lambda h: (0, h, 0, 0))
    return pl.pallas_call(
        _rec_decode_kernel,
        grid=(HEADS,),
        in_specs=[head, head, head, state],
        out_specs=[head, state],
        out_shape=[jax.ShapeDtypeStruct((n, D_MODEL), F32), jax.ShapeDtypeStruct(s0.shape, F32)],
        compiler_params=pltpu.CompilerParams(dimension_semantics=("parallel",), vmem_limit_bytes=VMEM_LIMIT),
        name="hgrn_recurrence_decode",
    )(q, f, v, s0)


def _mlp_residual(x, hn, wup_ref, wdn_ref):
    acc = x
    for c in range(D_FF // FF_CHUNK):
        cols = slice(c * FF_CHUNK, (c + 1) * FF_CHUNK)
        a = jnp.square(jnp.maximum(_dot(hn, wup_ref[:, cols]), 0.0)).astype(BF16)
        acc = acc + _dot(a, wdn_ref[cols, :])
    return acc


def _hgrn_out_mlp_kernel(o_ref, og_ref, x_ref, onorm_ref, wout_ref, gain_ref, wup_ref, wdn_ref, out_ref):
    heads = []
    for h in range(HEADS):
        oh = o_ref[:, h * HEAD_DIM:(h + 1) * HEAD_DIM]
        heads.append(oh * lax.rsqrt(jnp.mean(oh * oh, axis=-1, keepdims=True) + EPS))
    gated = (jnp.concatenate(heads, axis=-1) * onorm_ref[...] * og_ref[...]).astype(BF16)
    x1 = x_ref[...] + _dot(gated, wout_ref[...])
    out_ref[...] = _mlp_residual(x1, _rms(x1, gain_ref[...]).astype(BF16), wup_ref, wdn_ref)


def _hgrn_out_mlp(o, og, x, onorm, w_out, gain, w_up, w_dn, layer, tm):
    m = x.shape[0]
    row = pl.BlockSpec((tm, D_MODEL), lambda i: (i, 0))
    vec = _const_spec((1, D_MODEL))
    return pl.pallas_call(
        _hgrn_out_mlp_kernel,
        grid=(m // tm,),
        in_specs=[row, row, row, vec, _const_spec(w_out.shape), vec, _layer_spec(w_up.shape, layer),
                  _layer_spec(w_dn.shape, layer)],
        out_specs=row,
        out_shape=jax.ShapeDtypeStruct((m, D_MODEL), F32),
        compiler_params=pltpu.CompilerParams(dimension_semantics=("parallel",), vmem_limit_bytes=VMEM_LIMIT),
        name="hgrn_out_mlp",
    )(o, og, x, onorm, w_out, gain, w_up, w_dn)


def _pool_project(z, wgrp_ref, scale_ref):
    parts = [_dot(z[:, g * POOL_GW:(g + 1) * POOL_GW].astype(BF16), wgrp_ref[g]) for g in range(len(POOL_WINDOWS))]
    return jnp.concatenate(parts, axis=-1) * scale_ref[...]


def _shift_rows(a, k):
    return pltpu.roll(a, k, axis=0)


def _pool_prompt_kernel(x_ref, halo_ref, pgain_ref, wgrp_ref, scale_ref, gain_ref, wup_ref, wdn_ref, fgain_ref,
                        y_ref, hist_ref, x1_ref, hn_ref, *, tm, tiles_per_seq, n_tiles):
    s = pl.program_id(0)
    t = jnp.minimum(s, n_tiles - 1) % tiles_per_seq

    @pl.when(s == 0)
    def _():
        x1_ref[1] = jnp.zeros((tm, D_MODEL), F32)
        hn_ref[1] = jnp.zeros((tm, D_MODEL), BF16)

    assert D_FF // FF_CHUNK == len(POOL_WINDOWS)
    prev = (s + 1) % 2
    hn_prev = hn_ref[prev]
    acc = x1_ref[prev]

    x = x_ref[0]
    u = _rms(x, pgain_ref[...])
    halo = jnp.where(t > 0, _rms(halo_ref[0], pgain_ref[...]), 0.0)
    ext = jnp.concatenate([halo, u], axis=0)
    pos = t * tm + lax.broadcasted_iota(jnp.int32, (tm, 1), 0)
    parts = []
    for g, w in enumerate(POOL_WINDOWS):
        ff = slice(g * FF_CHUNK, (g + 1) * FF_CHUNK)
        a = jnp.square(jnp.maximum(_dot(hn_prev, wup_ref[:, ff]), 0.0)).astype(BF16)
        acc = acc + _dot(a, wdn_ref[ff, :])

        cols = slice(g * POOL_GW, (g + 1) * POOL_GW)
        win = ext[:, cols]
        k = 1
        while k < w:
            win = win + _shift_rows(win, k)
            k *= 2
        cnt = jnp.minimum(w, pos + 1).astype(F32)
        z = (win[HALO:, :] / cnt - u[:, cols]).astype(BF16)
        parts.append(x[:, cols] + _dot(z, wgrp_ref[g]) * scale_ref[:, cols])
    y_ref[0] = _rms(acc, fgain_ref[...])

    x1_new = jnp.concatenate(parts, axis=-1)
    cur = s % 2
    x1_ref[cur] = x1_new
    hn_ref[cur] = _rms(x1_new, gain_ref[...]).astype(BF16)

    @pl.when(t == tiles_per_seq - 1)
    def _():
        hist_ref[0] = ext[HALO + tm - POOL_HIST:, :]


def _pool_prompt(x, pgain, wgrp, scale, gain, w_up, w_dn, fgain, layer, tm):
    b, l, _ = x.shape
    vec = _const_spec((1, D_MODEL))
    tps = l // tm
    n_tiles = b * tps
    blocks_per_tile = tm // HALO
    cur = lambda s: jnp.minimum(s, n_tiles - 1)
    prev = lambda s: jnp.maximum(s - 1, 0)
    return pl.pallas_call(
        functools.partial(_pool_prompt_kernel, tm=tm, tiles_per_seq=tps, n_tiles=n_tiles),
        grid=(n_tiles + 1,),
        in_specs=[pl.BlockSpec((1, tm, D_MODEL), lambda s: (cur(s) // tps, cur(s) % tps, 0)),
                  pl.BlockSpec((1, HALO, D_MODEL),
                               lambda s: (cur(s) // tps, jnp.maximum((cur(s) % tps) * blocks_per_tile - 1, 0), 0)),
                  vec, _const_spec(wgrp.shape), vec, vec, _layer_spec(w_up.shape, layer),
                  _layer_spec(w_dn.shape, layer), vec],
        out_specs=[pl.BlockSpec((1, tm, D_MODEL), lambda s: (prev(s) // tps, prev(s) % tps, 0)),
                   pl.BlockSpec((1, POOL_HIST, D_MODEL), lambda s: (cur(s) // tps, 0, 0))],
        out_shape=[jax.ShapeDtypeStruct((b, l, D_MODEL), F32), jax.ShapeDtypeStruct((b, POOL_HIST, D_MODEL), F32)],
        scratch_shapes=[pltpu.VMEM((2, tm, D_MODEL), F32), pltpu.VMEM((2, tm, D_MODEL), BF16)],
        compiler_params=pltpu.CompilerParams(dimension_semantics=("arbitrary",), vmem_limit_bytes=VMEM_LIMIT),
        name="pool_mlp_prompt",
    )(x, x, pgain, wgrp, scale, gain, w_up, w_dn, fgain)


def _pool_decode_kernel(x_ref, hist_ref, pgain_ref, wgrp_ref, scale_ref, gain_ref, wup_ref, wdn_ref, fgain_ref,
                        y_ref, hist_out_ref):
    x = x_ref[...]
    u = _rms(x, pgain_ref[...])
    parts = []
    for g, w in enumerate(POOL_WINDOWS):
        cols = slice(g * POOL_GW, (g + 1) * POOL_GW)
        acc = u[:, cols]
        for j in range(1, w):
            acc = acc + hist_ref[POOL_HIST - j, :, cols]
        parts.append(acc / float(min(w, PAST_LEN + 1)) - u[:, cols])
    z = jnp.concatenate(parts, axis=-1)
    x1 = x + _pool_project(z, wgrp_ref, scale_ref)
    hn = _rms(x1, gain_ref[...]).astype(BF16)
    y_ref[...] = _rms(_mlp_residual(x1, hn, wup_ref, wdn_ref), fgain_ref[...])
    hist_out_ref[0:POOL_HIST - 1] = hist_ref[1:POOL_HIST]
    hist_out_ref[POOL_HIST - 1] = u


def _pool_decode(x, hist, pgain, wgrp, scale, gain, w_up, w_dn, fgain, layer):
    n = x.shape[0]
    vec = _const_spec((1, D_MODEL))
    return pl.pallas_call(
        _pool_decode_kernel,
        grid=(1,),
        in_specs=[_const_spec(x.shape), _const_spec(hist.shape), vec, _const_spec(wgrp.shape), vec, vec,
                  _layer_spec(w_up.shape, layer), _layer_spec(w_dn.shape, layer), vec],
        out_specs=[pl.BlockSpec(x.shape, lambda i: (0, 0)), pl.BlockSpec(hist.shape, lambda i: (0, 0, 0))],
        out_shape=[jax.ShapeDtypeStruct(x.shape, F32), jax.ShapeDtypeStruct(hist.shape, F32)],
        compiler_params=pltpu.CompilerParams(dimension_semantics=("arbitrary",), vmem_limit_bytes=VMEM_LIMIT),
        name="pool_mlp_decode",
    )(x, hist, pgain, wgrp, scale, gain, w_up, w_dn, fgain)


def kernel(x_prompt, x_sample, state_hgrn, state_pool, hgrn_norm, hgrn_w_in, hgrn_lb, hgrn_onorm, hgrn_w_out,
           pool_norm, pool_w, pool_scale, mlp_norm, mlp_up, mlp_down, final_norm):
    b, l, _ = x_prompt.shape
    n = x_sample.shape[0]
    vec = lambda a: a.reshape(1, D_MODEL)
    w_in = hgrn_w_in[0].astype(BF16)
    w_out = hgrn_w_out[0].astype(BF16)
    w_pool = pool_w[0].astype(BF16)
    w_up = mlp_up.astype(BF16)
    w_dn = mlp_down.astype(BF16)

    def layer0(x, tm):
        return _inproj(x, vec(hgrn_norm[0]), hgrn_lb, w_in, tm)

    def layer0_out(o, og, x, tm):
        return _hgrn_out_mlp(o, og, x, vec(hgrn_onorm[0]), w_out, vec(mlp_norm[0]), w_up, w_dn, 0, tm)

    pool_args = (vec(pool_norm[0]), w_pool, vec(pool_scale[0]), vec(mlp_norm[1]), w_up, w_dn, vec(final_norm), 1)

    xp = x_prompt.reshape(b * l, D_MODEL)
    q, f, v, og = layer0(xp, 512)
    seq = lambda a: a.reshape(b, l, D_MODEL)
    o, hgrn_p = _rec_prompt(seq(q), seq(f), seq(v), 512)
    x2 = layer0_out(o.reshape(b * l, D_MODEL), og, xp, 512)
    y_prompt, pool_p = _pool_prompt(seq(x2), *pool_args, tm=512)

    xs = x_sample.reshape(n, D_MODEL)
    q, f, v, og = layer0(xs, n)
    o, hgrn_s = _rec_decode(q, f, v, state_hgrn[0])
    x2 = layer0_out(o, og, xs, n)
    y_sample, pool_s = _pool_decode(x2, jnp.transpose(state_pool[0], (1, 0, 2)), *pool_args)

    return (y_prompt, y_sample.reshape(n, 1, D_MODEL), hgrn_p[None], hgrn_s[None], pool_p[None],
            jnp.transpose(pool_s, (1, 0, 2))[None])
```

```python
import functools

import jax
import jax.numpy as jnp
from jax import lax
from jax.experimental import pallas as pl
from jax.experimental.pallas import tpu as pltpu

D_MODEL = 1024
HEADS = 8
HEAD_DIM = 128
SUBLANES = 8
D_FF = 4 * D_MODEL
POOL_WINDOWS = (2, 4, 8, 16)
POOL_GW = D_MODEL // len(POOL_WINDOWS)
POOL_HIST = max(POOL_WINDOWS) - 1
HALO = 16
EPS = 1e-6
PAST_LEN = 16384

CHUNK = 128
LOG_DECAY_LIMIT = 80.0
FF_CHUNK = 1024
VMEM_LIMIT = 56 * 1024 * 1024

BF16 = jnp.bfloat16
F32 = jnp.float32


def _const_spec(shape):
    nd = len(shape)
    return pl.BlockSpec(shape, lambda *_: (0,) * nd, pipeline_mode=pl.Buffered(1))


def _layer_spec(shape, layer):
    return pl.BlockSpec((None,) + tuple(shape[1:]), lambda *_: (layer, 0, 0), pipeline_mode=pl.Buffered(1))


def _rms(x, gain):
    return x * lax.rsqrt(jnp.mean(x * x, axis=-1, keepdims=True) + EPS) * gain


def _sigmoid(x):
    return 1.0 / (1.0 + jnp.exp(-x))


def _dot(a, b):
    return jnp.dot(a, b, preferred_element_type=F32)


def _layer_lower_bound(lb_ref):
    lbr = lb_ref[...]
    e = jnp.exp(lbr - jnp.max(lbr, axis=0, keepdims=True))
    return e[0:1, :] / jnp.sum(e, axis=0, keepdims=True)


def _inproj_group(j, u, w_ref, lb):
    p = _dot(u, w_ref[:, j * D_MODEL:(j + 1) * D_MODEL])
    if j == 0:
        return p * _sigmoid(p) * (HEAD_DIM ** -0.5)
    if j == 1:
        return lb + (1.0 - lb) * _sigmoid(p)
    if j == 2:
        return p
    return _sigmoid(p)


def _inproj_kernel(x_ref, gain_ref, lb_ref, w_ref, q_ref, f_ref, v_ref, og_ref):
    u = _rms(x_ref[...], gain_ref[...]).astype(BF16)
    lb = _layer_lower_bound(lb_ref)
    q_ref[...] = _inproj_group(0, u, w_ref, lb)
    f_ref[...] = _inproj_group(1, u, w_ref, lb)
    v_ref[...] = _inproj_group(2, u, w_ref, lb).astype(BF16)
    og_ref[...] = _inproj_group(3, u, w_ref, lb)


def _inproj(x, gain, lb, w_in, tm):
    m = x.shape[0]
    row = pl.BlockSpec((tm, D_MODEL), lambda i: (i, 0))
    out = lambda dt: jax.ShapeDtypeStruct((m, D_MODEL), dt)
    return pl.pallas_call(
        _inproj_kernel,
        grid=(m // tm,),
        in_specs=[row, _const_spec((1, D_MODEL)), _const_spec(lb.shape), _const_spec(w_in.shape)],
        out_specs=[row] * 4,
        out_shape=[out(F32), out(F32), out(BF16), out(F32)],
        compiler_params=pltpu.CompilerParams(dimension_semantics=("parallel",), vmem_limit_bytes=VMEM_LIMIT),
        name="hgrn_inproj",
    )(x, gain, lb, w_in)


def _head_gate(o, onorm, og):
    return (o * lax.rsqrt(jnp.mean(o * o, axis=-1, keepdims=True) + EPS) * onorm * og).astype(BF16)


def _cumsum_rows(g):
    n = g.shape[0]
    row = lax.broadcasted_iota(jnp.int32, g.shape, 0)
    s = 1
    while s < n:
        g = g + jnp.where(row >= s, pltpu.roll(g, s, axis=0), 0.0)
        s *= 2
    return g


def _rec_chunk_factored(q, k, lg, v, st):
    tri = (lax.broadcasted_iota(jnp.int32, (CHUNK, CHUNK), 0)
           >= lax.broadcasted_iota(jnp.int32, (CHUNK, CHUNK), 1))
    big_g = _cumsum_rows(lg)
    decay = jnp.exp(big_g)
    qd = (q * decay).astype(BF16)
    kd = (k * jnp.exp(-big_g)).astype(BF16)
    a = lax.dot_general(qd, kd, (((1,), (1,)), ((), ())), preferred_element_type=F32)
    a = jnp.where(tri, a, 0.0).astype(BF16)
    o = _dot(a, v) + lax.dot_general(qd, st.astype(BF16), (((1,), (1,)), ((), ())), preferred_element_type=F32)
    ds = lax.dot_general(v, kd, (((0,), (0,)), ((), ())), preferred_element_type=F32)
    return o, (st + ds) * decay[CHUNK - 1:CHUNK, :]


def _rec_chunk_stepwise(q_ref, k_ref, lg_ref, v, st):
    vt = v.astype(F32).T
    lane = lax.broadcasted_iota(jnp.int32, (1, CHUNK), 1)

    def token_group(grp, carry):
        st, ot = carry
        rows = pl.ds(pl.multiple_of(grp * SUBLANES, SUBLANES), SUBLANES)
        lg, k, q = lg_ref[rows, :], k_ref[rows, :], q_ref[rows, :]
        for i in range(SUBLANES):
            onehot = (lane == grp * SUBLANES + i).astype(F32)
            vcol = jnp.sum(vt * onehot, axis=1, keepdims=True)
            st = st * jnp.exp(lg[i:i + 1, :]) + vcol * k[i:i + 1, :]
            ocol = jnp.sum(st * q[i:i + 1, :], axis=1, keepdims=True)
            ot = ot + ocol * onehot
        return st, ot

    st, ot = lax.fori_loop(0, CHUNK // SUBLANES, token_group, (st, jnp.zeros((HEAD_DIM, CHUNK), F32)))
    return ot.T, st


def _hgrn_mix_kernel(x_ref, gain_ref, lb_ref, w_ref, onorm_ref, gated_ref, s_out_ref,
                     q_s, k_s, lg_s, v_s, og_s, st_ref, slow_ref, *, tm, tiles_per_seq, n_tiles):
    s = pl.program_id(0)
    cur, prev = s % 2, (s + 1) % 2
    t_prev = jnp.maximum(s - 1, 0) % tiles_per_seq
    n_chunks = tm // CHUNK
    assert n_chunks == 4

    @pl.when(s == 0)
    def _():
        for ref in (q_s, k_s, lg_s, v_s, og_s):
            ref[1] = jnp.zeros(ref.shape[1:], ref.dtype)
        slow_ref[0] = 0

    @pl.when(t_prev == 0)
    def _():
        st_ref[...] = jnp.zeros_like(st_ref)

    def project(j, u, lb):
        val = _inproj_group(j, u, w_ref, lb)
        if j == 0:
            q_s[cur] = val
        elif j == 1:
            k_s[cur] = 1.0 - val
            lg_s[cur] = jnp.log(val)
        elif j == 2:
            v_s[cur] = val.astype(BF16)
        else:
            og_s[cur] = val

    def recur(c, stepwise):
        rows = slice(c * CHUNK, (c + 1) * CHUNK)
        for h in range(HEADS):
            cols = slice(h * HEAD_DIM, (h + 1) * HEAD_DIM)
            v = v_s[prev, rows, cols]
            if stepwise:
                o, st = _rec_chunk_stepwise(q_s.at[prev, rows, cols], k_s.at[prev, rows, cols],
                                            lg_s.at[prev, rows, cols], v, st_ref[h])
            else:
                o, st = _rec_chunk_factored(q_s[prev, rows, cols], k_s[prev, rows, cols], lg_s[prev, rows, cols],
                                            v, st_ref[h])
            st_ref[h] = st
            gated_ref[rows, cols] = _head_gate(o, onorm_ref[:, cols], og_s[prev, rows, cols])

    def body(stepwise):
        u = _rms(x_ref[...], gain_ref[...]).astype(BF16)
        lb = _layer_lower_bound(lb_ref)
        for j in range(4):
            project(j, u, lb)
            recur(j, stepwise)
        worst = None
        for c in range(n_chunks):
            tot = jnp.sum(lg_s[cur, c * CHUNK:(c + 1) * CHUNK, :], axis=0, keepdims=True)
            worst = tot if worst is None else jnp.minimum(worst, tot)
        slow_ref[0] = (jnp.min(worst) < -LOG_DECAY_LIMIT).astype(jnp.int32)

    lax.cond(slow_ref[0] == 0, lambda: body(False), lambda: body(True))

    @pl.when(t_prev == tiles_per_seq - 1)
    def _():
        for h in range(HEADS):
            s_out_ref[0, h] = st_ref[h].T


def _hgrn_mix(x, gain, lb, w_in, onorm, seqs, tm):
    m = x.shape[0]
    n_tiles = m // tm
    tps = n_tiles // seqs
    cur = lambda s: jnp.minimum(s, n_tiles - 1)
    prev = lambda s: jnp.maximum(s - 1, 0)
    vec = _const_spec((1, D_MODEL))
    buf = lambda dt: pltpu.VMEM((2, tm, D_MODEL), dt)
    return pl.pallas_call(
        functools.partial(_hgrn_mix_kernel, tm=tm, tiles_per_seq=tps, n_tiles=n_tiles),
        grid=(n_tiles + 1,),
        in_specs=[pl.BlockSpec((tm, D_MODEL), lambda s: (cur(s), 0)), vec, _const_spec(lb.shape),
                  _const_spec(w_in.shape), vec],
        out_specs=[pl.BlockSpec((tm, D_MODEL), lambda s: (prev(s), 0)),
                   pl.BlockSpec((1, HEADS, HEAD_DIM, HEAD_DIM), lambda s: (prev(s) // tps, 0, 0, 0))],
        out_shape=[jax.ShapeDtypeStruct((m, D_MODEL), BF16),
                   jax.ShapeDtypeStruct((seqs, HEADS, HEAD_DIM, HEAD_DIM), F32)],
        scratch_shapes=[buf(F32), buf(F32), buf(F32), buf(BF16), buf(F32),
                        pltpu.VMEM((HEADS, HEAD_DIM, HEAD_DIM), F32), pltpu.SMEM((1,), jnp.int32)],
        compiler_params=pltpu.CompilerParams(dimension_semantics=("arbitrary",), vmem_limit_bytes=VMEM_LIMIT),
        name="hgrn_mix_prompt",
    )(x, gain, lb, w_in, onorm)


def _rec_decode_kernel(q_ref, f_ref, v_ref, og_ref, onorm_ref, s_ref, gated_ref, s_out_ref, o_ref):
    n = q_ref.shape[0]
    qt = q_ref[...].T
    ft = f_ref[...].T
    v = v_ref[...].astype(F32)
    for b in range(n):
        fcol = ft[:, b:b + 1]
        s_new = fcol * s_ref[b, 0] + (1.0 - fcol) * v[b:b + 1, :]
        s_out_ref[b, 0] = s_new
        o_ref[b:b + 1, :] = jnp.sum(qt[:, b:b + 1] * s_new, axis=0, keepdims=True)
    gated_ref[...] = _head_gate(o_ref[...], onorm_ref[...], og_ref[...])


def _rec_decode(q, f, v, og, onorm, s0):
    n = q.shape[0]
    head = pl.BlockSpec((n, HEAD_DIM), lambda h: (0, h))
    state = pl.BlockSpec((n, 1, HEAD_DIM, HEAD_DIM), lambda h: (0, h, 0, 0))
    return pl.pallas_call(
        _rec_decode_kernel,
        grid=(HEADS,),
        in_specs=[head, head, head, head, pl.BlockSpec((1, HEAD_DIM), lambda h: (0, h)), state],
        out_specs=[head, state],
        out_shape=[jax.ShapeDtypeStruct((n, D_MODEL), BF16), jax.ShapeDtypeStruct(s0.shape, F32)],
        scratch_shapes=[pltpu.VMEM((n, HEAD_DIM), F32)],
        compiler_params=pltpu.CompilerParams(dimension_semantics=("parallel",), vmem_limit_bytes=VMEM_LIMIT),
        name="hgrn_recurrence_decode",
    )(q, f, v, og, onorm, s0)


def _mlp_residual(x, hn, wup_ref, wdn_ref):
    acc = x
    for c in range(D_FF // FF_CHUNK):
        cols = slice(c * FF_CHUNK, (c + 1) * FF_CHUNK)
        a = jnp.square(jnp.maximum(_dot(hn, wup_ref[:, cols]), 0.0)).astype(BF16)
        acc = acc + _dot(a, wdn_ref[cols, :])
    return acc


def _hgrn_out_mlp_kernel(gated_ref, x_ref, wout_ref, gain_ref, wup_ref, wdn_ref, out_ref):
    x1 = x_ref[...] + _dot(gated_ref[...], wout_ref[...])
    out_ref[...] = _mlp_residual(x1, _rms(x1, gain_ref[...]).astype(BF16), wup_ref, wdn_ref)


def _hgrn_out_mlp(gated, x, w_out, gain, w_up, w_dn, layer, tm):
    m = x.shape[0]
    row = pl.BlockSpec((tm, D_MODEL), lambda i: (i, 0))
    return pl.pallas_call(
        _hgrn_out_mlp_kernel,
        grid=(m // tm,),
        in_specs=[row, row, _const_spec(w_out.shape), _const_spec((1, D_MODEL)), _layer_spec(w_up.shape, layer),
                  _layer_spec(w_dn.shape, layer)],
        out_specs=row,
        out_shape=jax.ShapeDtypeStruct((m, D_MODEL), F32),
        compiler_params=pltpu.CompilerParams(dimension_semantics=("parallel",), vmem_limit_bytes=VMEM_LIMIT),
        name="hgrn_out_mlp",
    )(gated, x, w_out, gain, w_up, w_dn)


def _pool_project(z, wgrp_ref, scale_ref):
    parts = [_dot(z[:, g * POOL_GW:(g + 1) * POOL_GW].astype(BF16), wgrp_ref[g]) for g in range(len(POOL_WINDOWS))]
    return jnp.concatenate(parts, axis=-1) * scale_ref[...]


def _shift_rows(a, k):
    return pltpu.roll(a, k, axis=0)


def _pool_prompt_kernel(x_ref, halo_ref, pgain_ref, wgrp_ref, scale_ref, gain_ref, wup_ref, wdn_ref, fgain_ref,
                        y_ref, hist_ref, x1_ref, hn_ref, *, tm, tiles_per_seq, n_tiles):
    s = pl.program_id(0)
    t = jnp.minimum(s, n_tiles - 1) % tiles_per_seq

    @pl.when(s == 0)
    def _():
        x1_ref[1] = jnp.zeros((tm, D_MODEL), F32)
        hn_ref[1] = jnp.zeros((tm, D_MODEL), BF16)

    assert D_FF // FF_CHUNK == len(POOL_WINDOWS)
    prev = (s + 1) % 2
    hn_prev = hn_ref[prev]
    acc = x1_ref[prev]

    x = x_ref[0]
    u = _rms(x, pgain_ref[...])
    halo = jnp.where(t > 0, _rms(halo_ref[0], pgain_ref[...]), 0.0)
    ext = jnp.concatenate([halo, u], axis=0)
    pos = t * tm + lax.broadcasted_iota(jnp.int32, (tm, 1), 0)
    parts = []
    for g, w in enumerate(POOL_WINDOWS):
        ff = slice(g * FF_CHUNK, (g + 1) * FF_CHUNK)
        a = jnp.square(jnp.maximum(_dot(hn_prev, wup_ref[:, ff]), 0.0)).astype(BF16)
        acc = acc + _dot(a, wdn_ref[ff, :])

        cols = slice(g * POOL_GW, (g + 1) * POOL_GW)
        win = ext[:, cols]
        k = 1
        while k < w:
            win = win + _shift_rows(win, k)
            k *= 2
        cnt = jnp.minimum(w, pos + 1).astype(F32)
        z = (win[HALO:, :] / cnt - u[:, cols]).astype(BF16)
        parts.append(x[:, cols] + _dot(z, wgrp_ref[g]) * scale_ref[:, cols])
    y_ref[0] = _rms(acc, fgain_ref[...])

    x1_new = jnp.concatenate(parts, axis=-1)
    cur = s % 2
    x1_ref[cur] = x1_new
    hn_ref[cur] = _rms(x1_new, gain_ref[...]).astype(BF16)

    @pl.when(t == tiles_per_seq - 1)
    def _():
        hist_ref[0] = ext[HALO + tm - POOL_HIST:, :]


def _pool_prompt(x, pgain, wgrp, scale, gain, w_up, w_dn, fgain, layer, tm):
    b, l, _ = x.shape
    vec = _const_spec((1, D_MODEL))
    tps = l // tm
    n_tiles = b * tps
    blocks_per_tile = tm // HALO
    cur = lambda s: jnp.minimum(s, n_tiles - 1)
    prev = lambda s: jnp.maximum(s - 1, 0)
    return pl.pallas_call(
        functools.partial(_pool_prompt_kernel, tm=tm, tiles_per_seq=tps, n_tiles=n_tiles),
        grid=(n_tiles + 1,),
        in_specs=[pl.BlockSpec((1, tm, D_MODEL), lambda s: (cur(s) // tps, cur(s) % tps, 0)),
                  pl.BlockSpec((1, HALO, D_MODEL),
                               lambda s: (cur(s) // tps, jnp.maximum((cur(s) % tps) * blocks_per_tile - 1, 0), 0)),
                  vec, _const_spec(wgrp.shape), vec, vec, _layer_spec(w_up.shape, layer),
                  _layer_spec(w_dn.shape, layer), vec],
        out_specs=[pl.BlockSpec((1, tm, D_MODEL), lambda s: (prev(s) // tps, prev(s) % tps, 0)),
                   pl.BlockSpec((1, POOL_HIST, D_MODEL), lambda s: (cur(s) // tps, 0, 0))],
        out_shape=[jax.ShapeDtypeStruct((b, l, D_MODEL), F32), jax.ShapeDtypeStruct((b, POOL_HIST, D_MODEL), F32)],
        scratch_shapes=[pltpu.VMEM((2, tm, D_MODEL), F32), pltpu.VMEM((2, tm, D_MODEL), BF16)],
        compiler_params=pltpu.CompilerParams(dimension_semantics=("arbitrary",), vmem_limit_bytes=VMEM_LIMIT),
        name="pool_mlp_prompt",
    )(x, x, pgain, wgrp, scale, gain, w_up, w_dn, fgain)


def _pool_decode_kernel(x_ref, hist_ref, pgain_ref, wgrp_ref, scale_ref, gain_ref, wup_ref, wdn_ref, fgain_ref,
                        y_ref, hist_out_ref):
    x = x_ref[...]
    u = _rms(x, pgain_ref[...])
    parts = []
    for g, w in enumerate(POOL_WINDOWS):
        cols = slice(g * POOL_GW, (g + 1) * POOL_GW)
        acc = u[:, cols]
        for j in range(1, w):
            acc = acc + hist_ref[POOL_HIST - j, :, cols]
        parts.append(acc / float(min(w, PAST_LEN + 1)) - u[:, cols])
    z = jnp.concatenate(parts, axis=-1)
    x1 = x + _pool_project(z, wgrp_ref, scale_ref)
    hn = _rms(x1, gain_ref[...]).astype(BF16)
    y_ref[...] = _rms(_mlp_residual(x1, hn, wup_ref, wdn_ref), fgain_ref[...])
    hist_out_ref[0:POOL_HIST - 1] = hist_ref[1:POOL_HIST]
    hist_out_ref[POOL_HIST - 1] = u


def _pool_decode(x, hist, pgain, wgrp, scale, gain, w_up, w_dn, fgain, layer):
    n = x.shape[0]
    vec = _const_spec((1, D_MODEL))
    return pl.pallas_call(
        _pool_decode_kernel,
        grid=(1,),
        in_specs=[_const_spec(x.shape), _const_spec(hist.shape), vec, _const_spec(wgrp.shape), vec, vec,
                  _layer_spec(w_up.shape, layer), _layer_spec(w_dn.shape, layer), vec],
        out_specs=[pl.BlockSpec(x.shape, lambda i: (0, 0)), pl.BlockSpec(hist.shape, lambda i: (0, 0, 0))],
        out_shape=[jax.ShapeDtypeStruct(x.shape, F32), jax.ShapeDtypeStruct(hist.shape, F32)],
        compiler_params=pltpu.CompilerParams(dimension_semantics=("arbitrary",), vmem_limit_bytes=VMEM_LIMIT),
        name="pool_mlp_decode",
    )(x, hist, pgain, wgrp, scale, gain, w_up, w_dn, fgain)


def kernel(x_prompt, x_sample, state_hgrn, state_pool, hgrn_norm, hgrn_w_in, hgrn_lb, hgrn_onorm, hgrn_w_out,
           pool_norm, pool_w, pool_scale, mlp_norm, mlp_up, mlp_down, final_norm):
    b, l, _ = x_prompt.shape
    n = x_sample.shape[0]
    vec = lambda a: a.reshape(1, D_MODEL)
    w_in = hgrn_w_in[0].astype(BF16)
    w_out = hgrn_w_out[0].astype(BF16)
    w_pool = pool_w[0].astype(BF16)
    w_up = mlp_up.astype(BF16)
    w_dn = mlp_down.astype(BF16)

    pool_args = (vec(pool_norm[0]), w_pool, vec(pool_scale[0]), vec(mlp_norm[1]), w_up, w_dn, vec(final_norm), 1)
    onorm = vec(hgrn_onorm[0])

    def layer0_out(gated, x, tm):
        return _hgrn_out_mlp(gated, x, w_out, vec(mlp_norm[0]), w_up, w_dn, 0, tm)

    xp = x_prompt.reshape(b * l, D_MODEL)
    gated, hgrn_p = _hgrn_mix(xp, vec(hgrn_norm[0]), hgrn_lb, w_in, onorm, b, 512)
    x2 = layer0_out(gated, xp, 512)
    y_prompt, pool_p = _pool_prompt(x2.reshape(b, l, D_MODEL), *pool_args, tm=512)

    xs = x_sample.reshape(n, D_MODEL)
    q, f, v, og = _inproj(xs, vec(hgrn_norm[0]), hgrn_lb, w_in, n)
    gated, hgrn_s = _rec_decode(q, f, v, og, onorm, state_hgrn[0])
    x2 = layer0_out(gated, xs, n)
    y_sample, pool_s = _pool_decode(x2, jnp.transpose(state_pool[0], (1, 0, 2)), *pool_args)

    return (y_prompt, y_sample.reshape(n, 1, D_MODEL), hgrn_p[None], hgrn_s[None], pool_p[None],
            jnp.transpose(pool_s, (1, 0, 2))[None])
```

```python
import functools

import jax
import jax.numpy as jnp
from jax import lax
from jax.experimental import pallas as pl
from jax.experimental.pallas import tpu as pltpu

D_MODEL = 1024
HEADS = 8
HEAD_DIM = 128
SUBLANES = 8
D_FF = 4 * D_MODEL
POOL_WINDOWS = (2, 4, 8, 16)
POOL_GW = D_MODEL // len(POOL_WINDOWS)
POOL_HIST = max(POOL_WINDOWS) - 1
HALO = 16
EPS = 1e-6
PAST_LEN = 16384

CHUNK = 128
LOG_DECAY_LIMIT = 80.0
FF_CHUNK = 1024
VMEM_LIMIT = 56 * 1024 * 1024

BF16 = jnp.bfloat16
F32 = jnp.float32


def _const_spec(shape):
    nd = len(shape)
    return pl.BlockSpec(shape, lambda *_: (0,) * nd, pipeline_mode=pl.Buffered(1))


def _rms(x, gain):
    return x * lax.rsqrt(jnp.mean(x * x, axis=-1, keepdims=True) + EPS) * gain


def _sigmoid(x):
    return 1.0 / (1.0 + jnp.exp(-x))


def _dot(a, b):
    return jnp.dot(a, b, preferred_element_type=F32)


def _layer_lower_bound(lb_ref):
    lbr = lb_ref[...]
    e = jnp.exp(lbr - jnp.max(lbr, axis=0, keepdims=True))
    return e[0:1, :] / jnp.sum(e, axis=0, keepdims=True)


def _inproj_group(j, u, w_ref, lb):
    p = _dot(u, w_ref[:, j * D_MODEL:(j + 1) * D_MODEL])
    if j == 0:
        return p * _sigmoid(p) * (HEAD_DIM ** -0.5)
    if j == 1:
        return lb + (1.0 - lb) * _sigmoid(p)
    if j == 2:
        return p
    return _sigmoid(p)


def _inproj_kernel(x_ref, gain_ref, lb_ref, w_ref, q_ref, f_ref, v_ref, og_ref):
    u = _rms(x_ref[...], gain_ref[...]).astype(BF16)
    lb = _layer_lower_bound(lb_ref)
    q_ref[...] = _inproj_group(0, u, w_ref, lb)
    f_ref[...] = _inproj_group(1, u, w_ref, lb)
    v_ref[...] = _inproj_group(2, u, w_ref, lb).astype(BF16)
    og_ref[...] = _inproj_group(3, u, w_ref, lb)


def _inproj(x, gain, lb, w_in, tm):
    m = x.shape[0]
    row = pl.BlockSpec((tm, D_MODEL), lambda i: (i, 0))
    out = lambda dt: jax.ShapeDtypeStruct((m, D_MODEL), dt)
    return pl.pallas_call(
        _inproj_kernel,
        grid=(m // tm,),
        in_specs=[row, _const_spec((1, D_MODEL)), _const_spec(lb.shape), _const_spec(w_in.shape)],
        out_specs=[row] * 4,
        out_shape=[out(F32), out(F32), out(BF16), out(F32)],
        compiler_params=pltpu.CompilerParams(dimension_semantics=("parallel",), vmem_limit_bytes=VMEM_LIMIT),
        name="hgrn_inproj",
    )(x, gain, lb, w_in)


def _head_gate(o, onorm, og):
    return (o * lax.rsqrt(jnp.mean(o * o, axis=-1, keepdims=True) + EPS) * onorm * og).astype(BF16)


def _lower_triangle():
    return (lax.broadcasted_iota(jnp.int32, (CHUNK, CHUNK), 0)
            >= lax.broadcasted_iota(jnp.int32, (CHUNK, CHUNK), 1))


def _cumsum_rows(g):
    n = g.shape[0]
    row = lax.broadcasted_iota(jnp.int32, g.shape, 0)
    s = 1
    while s < n:
        g = g + jnp.where(row >= s, pltpu.roll(g, s, axis=0), 0.0)
        s *= 2
    return g


def _rec_chunk_factored(q, k, big_g, v, st):
    tri = _lower_triangle()
    decay = jnp.exp(big_g)
    qd = (q * decay).astype(BF16)
    kd = (k * jnp.exp(-big_g)).astype(BF16)
    a = lax.dot_general(qd, kd, (((1,), (1,)), ((), ())), preferred_element_type=F32)
    a = jnp.where(tri, a, 0.0).astype(BF16)
    o = _dot(a, v) + lax.dot_general(qd, st.astype(BF16), (((1,), (1,)), ((), ())), preferred_element_type=F32)
    ds = lax.dot_general(v, kd, (((0,), (0,)), ((), ())), preferred_element_type=F32)
    return o, (st + ds) * decay[CHUNK - 1:CHUNK, :]


def _rec_chunk_stepwise(q_ref, k_ref, lg_ref, v, st):
    vt = v.astype(F32).T
    lane = lax.broadcasted_iota(jnp.int32, (1, CHUNK), 1)

    def token_group(grp, carry):
        st, ot = carry
        rows = pl.ds(pl.multiple_of(grp * SUBLANES, SUBLANES), SUBLANES)
        lg, k, q = lg_ref[rows, :], k_ref[rows, :], q_ref[rows, :]
        for i in range(SUBLANES):
            onehot = (lane == grp * SUBLANES + i).astype(F32)
            vcol = jnp.sum(vt * onehot, axis=1, keepdims=True)
            st = st * jnp.exp(lg[i:i + 1, :]) + vcol * k[i:i + 1, :]
            ocol = jnp.sum(st * q[i:i + 1, :], axis=1, keepdims=True)
            ot = ot + ocol * onehot
        return st, ot

    st, ot = lax.fori_loop(0, CHUNK // SUBLANES, token_group, (st, jnp.zeros((HEAD_DIM, CHUNK), F32)))
    return ot.T, st


def _cast_slabs(src_refs, dst_refs):
    for src, dst in zip(src_refs, dst_refs):
        dst[...] = src[...].astype(BF16)


def _slab_specs(weights, n_steps):
    ins, outs, shapes = [], [], []
    for w, layer in weights:
        _, rows, cols = w.shape
        slab = rows // n_steps
        assert slab * n_steps == rows and slab % 16 == 0
        ins.append(pl.BlockSpec((None, slab, cols), lambda s, layer=layer: (layer, jnp.minimum(s, n_steps - 1), 0)))
        outs.append(pl.BlockSpec((slab, cols), lambda s: (jnp.minimum(s, n_steps - 1), 0)))
        shapes.append(jax.ShapeDtypeStruct((rows, cols), BF16))
    return ins, outs, shapes


def _hgrn_mix_kernel(x_ref, gain_ref, lb_ref, w_ref, onorm_ref, wa_ref, wb_ref, wc_ref,
                     gated_ref, s_out_ref, wa_out, wb_out, wc_out,
                     q_s, k_s, lg_s, v_s, og_s, st_ref, slow_ref, *, tm, tiles_per_seq, n_tiles):
    s = pl.program_id(0)
    _cast_slabs((wa_ref, wb_ref, wc_ref), (wa_out, wb_out, wc_out))
    cur, prev = s % 2, (s + 1) % 2
    t_prev = jnp.maximum(s - 1, 0) % tiles_per_seq
    n_chunks = tm // CHUNK
    assert n_chunks == 4

    @pl.when(s == 0)
    def _():
        for ref in (q_s, k_s, lg_s, v_s, og_s):
            ref[1] = jnp.zeros(ref.shape[1:], ref.dtype)
        slow_ref[0] = 0

    @pl.when(t_prev == 0)
    def _():
        st_ref[...] = jnp.zeros_like(st_ref)

    def project(j, u, lb):
        val = _inproj_group(j, u, w_ref, lb)
        if j == 0:
            q_s[cur] = val
        elif j == 1:
            k_s[cur] = 1.0 - val
            lg_s[cur] = jnp.log(val)
        elif j == 2:
            v_s[cur] = val.astype(BF16)
        else:
            og_s[cur] = val

    def recur(c, stepwise):
        rows = slice(c * CHUNK, (c + 1) * CHUNK)
        if not stepwise:
            big_g = _cumsum_rows(lg_s[prev, rows, :])
        for h in range(HEADS):
            cols = slice(h * HEAD_DIM, (h + 1) * HEAD_DIM)
            v = v_s[prev, rows, cols]
            if stepwise:
                o, st = _rec_chunk_stepwise(q_s.at[prev, rows, cols], k_s.at[prev, rows, cols],
                                            lg_s.at[prev, rows, cols], v, st_ref[h])
            else:
                o, st = _rec_chunk_factored(q_s[prev, rows, cols], k_s[prev, rows, cols], big_g[:, cols], v, st_ref[h])
            st_ref[h] = st
            gated_ref[rows, cols] = _head_gate(o, onorm_ref[:, cols], og_s[prev, rows, cols])

    def body(stepwise):
        u = _rms(x_ref[...], gain_ref[...]).astype(BF16)
        lb = _layer_lower_bound(lb_ref)
        for j in range(4):
            project(j, u, lb)
            recur(j, stepwise)
        worst = None
        for c in range(n_chunks):
            tot = jnp.sum(lg_s[cur, c * CHUNK:(c + 1) * CHUNK, :], axis=0, keepdims=True)
            worst = tot if worst is None else jnp.minimum(worst, tot)
        slow_ref[0] = (jnp.min(worst) < -LOG_DECAY_LIMIT).astype(jnp.int32)

    lax.cond(slow_ref[0] == 0, lambda: body(False), lambda: body(True))

    @pl.when(t_prev == tiles_per_seq - 1)
    def _():
        for h in range(HEADS):
            s_out_ref[0, h] = st_ref[h].T


def _hgrn_mix(x, gain, lb, w_in, onorm, later_weights, seqs, tm):
    m = x.shape[0]
    n_tiles = m // tm
    w_ins, w_outs, w_shapes = _slab_specs(later_weights, n_tiles)
    tps = n_tiles // seqs
    cur = lambda s: jnp.minimum(s, n_tiles - 1)
    prev = lambda s: jnp.maximum(s - 1, 0)
    vec = _const_spec((1, D_MODEL))
    buf = lambda dt: pltpu.VMEM((2, tm, D_MODEL), dt)
    return pl.pallas_call(
        functools.partial(_hgrn_mix_kernel, tm=tm, tiles_per_seq=tps, n_tiles=n_tiles),
        grid=(n_tiles + 1,),
        in_specs=[pl.BlockSpec((tm, D_MODEL), lambda s: (cur(s), 0)), vec, _const_spec(lb.shape),
                  _const_spec(w_in.shape), vec] + w_ins,
        out_specs=[pl.BlockSpec((tm, D_MODEL), lambda s: (prev(s), 0)),
                   pl.BlockSpec((1, HEADS, HEAD_DIM, HEAD_DIM), lambda s: (prev(s) // tps, 0, 0, 0))] + w_outs,
        out_shape=[jax.ShapeDtypeStruct((m, D_MODEL), BF16),
                   jax.ShapeDtypeStruct((seqs, HEADS, HEAD_DIM, HEAD_DIM), F32)] + w_shapes,
        scratch_shapes=[buf(F32), buf(F32), buf(F32), buf(BF16), buf(F32),
                        pltpu.VMEM((HEADS, HEAD_DIM, HEAD_DIM), F32), pltpu.SMEM((1,), jnp.int32)],
        compiler_params=pltpu.CompilerParams(dimension_semantics=("arbitrary",), vmem_limit_bytes=VMEM_LIMIT),
        name="hgrn_mix_prompt",
    )(x, gain, lb, w_in, onorm, *[w for w, _ in later_weights])


def _rec_decode_kernel(q_ref, f_ref, v_ref, og_ref, onorm_ref, s_ref, gated_ref, s_out_ref, o_ref):
    n = q_ref.shape[0]
    qt = q_ref[...].T
    ft = f_ref[...].T
    v = v_ref[...].astype(F32)
    for b in range(n):
        fcol = ft[:, b:b + 1]
        s_new = fcol * s_ref[b, 0] + (1.0 - fcol) * v[b:b + 1, :]
        s_out_ref[b, 0] = s_new
        o_ref[b:b + 1, :] = jnp.sum(qt[:, b:b + 1] * s_new, axis=0, keepdims=True)
    gated_ref[...] = _head_gate(o_ref[...], onorm_ref[...], og_ref[...])


def _rec_decode(q, f, v, og, onorm, s0):
    n = q.shape[0]
    head = pl.BlockSpec((n, HEAD_DIM), lambda h: (0, h))
    state = pl.BlockSpec((n, 1, HEAD_DIM, HEAD_DIM), lambda h: (0, h, 0, 0))
    return pl.pallas_call(
        _rec_decode_kernel,
        grid=(HEADS,),
        in_specs=[head, head, head, head, pl.BlockSpec((1, HEAD_DIM), lambda h: (0, h)), state],
        out_specs=[head, state],
        out_shape=[jax.ShapeDtypeStruct((n, D_MODEL), BF16), jax.ShapeDtypeStruct(s0.shape, F32)],
        scratch_shapes=[pltpu.VMEM((n, HEAD_DIM), F32)],
        compiler_params=pltpu.CompilerParams(dimension_semantics=("parallel",), vmem_limit_bytes=VMEM_LIMIT),
        name="hgrn_recurrence_decode",
    )(q, f, v, og, onorm, s0)


def _mlp_residual(x, hn, wup_ref, wdn_ref):
    acc = x
    for c in range(D_FF // FF_CHUNK):
        cols = slice(c * FF_CHUNK, (c + 1) * FF_CHUNK)
        a = jnp.square(jnp.maximum(_dot(hn, wup_ref[:, cols]), 0.0)).astype(BF16)
        acc = acc + _dot(a, wdn_ref[cols, :])
    return acc


def _hgrn_out_mlp_kernel(gated_ref, x_ref, wout_ref, gain_ref, wup_ref, wdn_ref, *rest):
    n_cast = (len(rest) - 1) // 2
    _cast_slabs(rest[:n_cast], rest[n_cast + 1:])
    out_ref = rest[n_cast]
    x1 = x_ref[...] + _dot(gated_ref[...], wout_ref[...])
    out_ref[...] = _mlp_residual(x1, _rms(x1, gain_ref[...]).astype(BF16), wup_ref, wdn_ref)


def _hgrn_out_mlp(gated, x, w_out, gain, w_up, w_dn, tm, later_weights=()):
    m = x.shape[0]
    row = pl.BlockSpec((tm, D_MODEL), lambda i: (i, 0))
    w_ins, w_outs, w_shapes = _slab_specs(later_weights, m // tm)
    return pl.pallas_call(
        _hgrn_out_mlp_kernel,
        grid=(m // tm,),
        in_specs=[row, row, _const_spec(w_out.shape), _const_spec((1, D_MODEL)), _const_spec(w_up.shape),
                  _const_spec(w_dn.shape)] + w_ins,
        out_specs=[row] + w_outs,
        out_shape=[jax.ShapeDtypeStruct((m, D_MODEL), F32)] + w_shapes,
        compiler_params=pltpu.CompilerParams(dimension_semantics=("arbitrary",), vmem_limit_bytes=VMEM_LIMIT),
        name="hgrn_out_mlp",
    )(gated, x, w_out, gain, w_up, w_dn, *[w for w, _ in later_weights])


def _pool_project(z, wgrp_ref, scale_ref):
    parts = [_dot(z[:, g * POOL_GW:(g + 1) * POOL_GW].astype(BF16), wgrp_ref[g]) for g in range(len(POOL_WINDOWS))]
    return jnp.concatenate(parts, axis=-1) * scale_ref[...]


def _shift_rows(a, k):
    return pltpu.roll(a, k, axis=0)


def _pool_prompt_kernel(x_ref, halo_ref, pgain_ref, wgrp_ref, scale_ref, gain_ref, wup_ref, wdn_ref, fgain_ref,
                        y_ref, hist_ref, x1_ref, hn_ref, *, tm, tiles_per_seq, n_tiles):
    s = pl.program_id(0)
    t = jnp.minimum(s, n_tiles - 1) % tiles_per_seq

    @pl.when(s == 0)
    def _():
        x1_ref[1] = jnp.zeros((tm, D_MODEL), F32)
        hn_ref[1] = jnp.zeros((tm, D_MODEL), BF16)

    assert D_FF // FF_CHUNK == len(POOL_WINDOWS)
    prev = (s + 1) % 2
    hn_prev = hn_ref[prev]
    acc = x1_ref[prev]

    x = x_ref[0]
    u = _rms(x, pgain_ref[...])
    halo = jnp.where(t > 0, _rms(halo_ref[0], pgain_ref[...]), 0.0)
    ext = jnp.concatenate([halo, u], axis=0)
    pos = t * tm + lax.broadcasted_iota(jnp.int32, (tm, 1), 0)
    parts = []
    for g, w in enumerate(POOL_WINDOWS):
        ff = slice(g * FF_CHUNK, (g + 1) * FF_CHUNK)
        a = jnp.square(jnp.maximum(_dot(hn_prev, wup_ref[:, ff]), 0.0)).astype(BF16)
        acc = acc + _dot(a, wdn_ref[ff, :])

        cols = slice(g * POOL_GW, (g + 1) * POOL_GW)
        win = ext[:, cols]
        k = 1
        while k < w:
            win = win + _shift_rows(win, k)
            k *= 2
        cnt = jnp.minimum(w, pos + 1).astype(F32)
        z = (win[HALO:, :] / cnt - u[:, cols]).astype(BF16)
        parts.append(x[:, cols] + _dot(z, wgrp_ref[g]) * scale_ref[:, cols])
    y_ref[0] = _rms(acc, fgain_ref[...])

    x1_new = jnp.concatenate(parts, axis=-1)
    cur = s % 2
    x1_ref[cur] = x1_new
    hn_ref[cur] = _rms(x1_new, gain_ref[...]).astype(BF16)

    @pl.when(t == tiles_per_seq - 1)
    def _():
        hist_ref[0] = ext[HALO + tm - POOL_HIST:, :]


def _pool_prompt(x, pgain, wgrp, scale, gain, w_up, w_dn, fgain, tm):
    b, l, _ = x.shape
    vec = _const_spec((1, D_MODEL))
    tps = l // tm
    n_tiles = b * tps
    blocks_per_tile = tm // HALO
    cur = lambda s: jnp.minimum(s, n_tiles - 1)
    prev = lambda s: jnp.maximum(s - 1, 0)
    return pl.pallas_call(
        functools.partial(_pool_prompt_kernel, tm=tm, tiles_per_seq=tps, n_tiles=n_tiles),
        grid=(n_tiles + 1,),
        in_specs=[pl.BlockSpec((1, tm, D_MODEL), lambda s: (cur(s) // tps, cur(s) % tps, 0)),
                  pl.BlockSpec((1, HALO, D_MODEL),
                               lambda s: (cur(s) // tps, jnp.maximum((cur(s) % tps) * blocks_per_tile - 1, 0), 0)),
                  vec, _const_spec(wgrp.shape), vec, vec, _const_spec(w_up.shape), _const_spec(w_dn.shape), vec],
        out_specs=[pl.BlockSpec((1, tm, D_MODEL), lambda s: (prev(s) // tps, prev(s) % tps, 0)),
                   pl.BlockSpec((1, POOL_HIST, D_MODEL), lambda s: (cur(s) // tps, 0, 0))],
        out_shape=[jax.ShapeDtypeStruct((b, l, D_MODEL), F32), jax.ShapeDtypeStruct((b, POOL_HIST, D_MODEL), F32)],
        scratch_shapes=[pltpu.VMEM((2, tm, D_MODEL), F32), pltpu.VMEM((2, tm, D_MODEL), BF16)],
        compiler_params=pltpu.CompilerParams(dimension_semantics=("arbitrary",), vmem_limit_bytes=VMEM_LIMIT),
        name="pool_mlp_prompt",
    )(x, x, pgain, wgrp, scale, gain, w_up, w_dn, fgain)


def _pool_decode_kernel(x_ref, hist_ref, pgain_ref, wgrp_ref, scale_ref, gain_ref, wup_ref, wdn_ref, fgain_ref,
                        y_ref, hist_out_ref):
    x = x_ref[...]
    u = _rms(x, pgain_ref[...])
    parts = []
    for g, w in enumerate(POOL_WINDOWS):
        cols = slice(g * POOL_GW, (g + 1) * POOL_GW)
        acc = u[:, cols]
        for j in range(1, w):
            acc = acc + hist_ref[POOL_HIST - j, :, cols]
        parts.append(acc / float(min(w, PAST_LEN + 1)) - u[:, cols])
    z = jnp.concatenate(parts, axis=-1)
    x1 = x + _pool_project(z, wgrp_ref, scale_ref)
    hn = _rms(x1, gain_ref[...]).astype(BF16)
    y_ref[...] = _rms(_mlp_residual(x1, hn, wup_ref, wdn_ref), fgain_ref[...])
    hist_out_ref[0:POOL_HIST - 1] = hist_ref[1:POOL_HIST]
    hist_out_ref[POOL_HIST - 1] = u


def _pool_decode(x, hist, pgain, wgrp, scale, gain, w_up, w_dn, fgain):
    n = x.shape[0]
    vec = _const_spec((1, D_MODEL))
    return pl.pallas_call(
        _pool_decode_kernel,
        grid=(1,),
        in_specs=[_const_spec(x.shape), _const_spec(hist.shape), vec, _const_spec(wgrp.shape), vec, vec,
                  _const_spec(w_up.shape), _const_spec(w_dn.shape), vec],
        out_specs=[pl.BlockSpec(x.shape, lambda i: (0, 0)), pl.BlockSpec(hist.shape, lambda i: (0, 0, 0))],
        out_shape=[jax.ShapeDtypeStruct(x.shape, F32), jax.ShapeDtypeStruct(hist.shape, F32)],
        compiler_params=pltpu.CompilerParams(dimension_semantics=("arbitrary",), vmem_limit_bytes=VMEM_LIMIT),
        name="pool_mlp_decode",
    )(x, hist, pgain, wgrp, scale, gain, w_up, w_dn, fgain)


def kernel(x_prompt, x_sample, state_hgrn, state_pool, hgrn_norm, hgrn_w_in, hgrn_lb, hgrn_onorm, hgrn_w_out,
           pool_norm, pool_w, pool_scale, mlp_norm, mlp_up, mlp_down, final_norm):
    b, l, _ = x_prompt.shape
    n = x_sample.shape[0]
    vec = lambda a: a.reshape(1, D_MODEL)
    w_in = hgrn_w_in[0].astype(BF16)
    onorm = vec(hgrn_onorm[0])

    xp = x_prompt.reshape(b * l, D_MODEL)
    gated, hgrn_p, w_up0, w_dn0, w_out = _hgrn_mix(xp, vec(hgrn_norm[0]), hgrn_lb, w_in, onorm,
                                                   ((mlp_up, 0), (mlp_down, 0), (hgrn_w_out, 0)), b, 512)
    pool_w_rows = pool_w.reshape(pool_w.shape[0], D_MODEL, POOL_GW)
    x2, w_up1, w_dn1, w_pool = _hgrn_out_mlp(gated, xp, w_out, vec(mlp_norm[0]), w_up0, w_dn0, 512,
                                             ((mlp_up, 1), (mlp_down, 1), (pool_w_rows, 0)))
    w_pool = w_pool.reshape(len(POOL_WINDOWS), POOL_GW, POOL_GW)
    pool_args = (vec(pool_norm[0]), w_pool, vec(pool_scale[0]), vec(mlp_norm[1]), w_up1, w_dn1, vec(final_norm))
    y_prompt, pool_p = _pool_prompt(x2.reshape(b, l, D_MODEL), *pool_args, tm=512)

    xs = x_sample.reshape(n, D_MODEL)
    q, f, v, og = _inproj(xs, vec(hgrn_norm[0]), hgrn_lb, w_in, n)
    gated, hgrn_s = _rec_decode(q, f, v, og, onorm, state_hgrn[0])
    x2, = _hgrn_out_mlp(gated, xs, w_out, vec(mlp_norm[0]), w_up0, w_dn0, n)
    y_sample, pool_s = _pool_decode(x2, jnp.transpose(state_pool[0], (1, 0, 2)), *pool_args)

    return (y_prompt, y_sample.reshape(n, 1, D_MODEL), hgrn_p[None], hgrn_s[None], pool_p[None],
            jnp.transpose(pool_s, (1, 0, 2))[None])
```

```python
import functools

import jax
import jax.numpy as jnp
from jax import lax
from jax.experimental import pallas as pl
from jax.experimental.pallas import tpu as pltpu

D_MODEL = 1024
HEADS = 8
HEAD_DIM = 128
SUBLANES = 8
D_FF = 4 * D_MODEL
POOL_WINDOWS = (2, 4, 8, 16)
POOL_GW = D_MODEL // len(POOL_WINDOWS)
POOL_HIST = max(POOL_WINDOWS) - 1
HALO = 16
EPS = 1e-6
PAST_LEN = 16384

CHUNK = 128
LOG_DECAY_LIMIT = 80.0
FF_CHUNK = 1024
VMEM_LIMIT = 56 * 1024 * 1024

BF16 = jnp.bfloat16
F32 = jnp.float32


def _const_spec(shape):
    nd = len(shape)
    return pl.BlockSpec(shape, lambda *_: (0,) * nd, pipeline_mode=pl.Buffered(1))


def _rms(x, gain):
    return x * lax.rsqrt(jnp.mean(x * x, axis=-1, keepdims=True) + EPS) * gain


def _sigmoid(x):
    return 1.0 / (1.0 + jnp.exp(-x))


def _dot(a, b):
    return jnp.dot(a, b, preferred_element_type=F32)


def _layer_lower_bound(lb_ref):
    lbr = lb_ref[...]
    e = jnp.exp(lbr - jnp.max(lbr, axis=0, keepdims=True))
    return e[0:1, :] / jnp.sum(e, axis=0, keepdims=True)


def _inproj_group(j, u, w_ref, lb):
    p = _dot(u, w_ref[:, j * D_MODEL:(j + 1) * D_MODEL])
    if j == 0:
        return p * _sigmoid(p) * (HEAD_DIM ** -0.5)
    if j == 1:
        return lb + (1.0 - lb) * _sigmoid(p)
    if j == 2:
        return p
    return _sigmoid(p)


def _inproj_kernel(x_ref, gain_ref, lb_ref, w_ref, q_ref, f_ref, v_ref, og_ref):
    u = _rms(x_ref[...], gain_ref[...]).astype(BF16)
    lb = _layer_lower_bound(lb_ref)
    q_ref[...] = _inproj_group(0, u, w_ref, lb)
    f_ref[...] = _inproj_group(1, u, w_ref, lb)
    v_ref[...] = _inproj_group(2, u, w_ref, lb).astype(BF16)
    og_ref[...] = _inproj_group(3, u, w_ref, lb)


def _inproj(x, gain, lb, w_in, tm):
    m = x.shape[0]
    row = pl.BlockSpec((tm, D_MODEL), lambda i: (i, 0))
    out = lambda dt: jax.ShapeDtypeStruct((m, D_MODEL), dt)
    return pl.pallas_call(
        _inproj_kernel,
        grid=(m // tm,),
        in_specs=[row, _const_spec((1, D_MODEL)), _const_spec(lb.shape), _const_spec(w_in.shape)],
        out_specs=[row] * 4,
        out_shape=[out(F32), out(F32), out(BF16), out(F32)],
        compiler_params=pltpu.CompilerParams(dimension_semantics=("parallel",), vmem_limit_bytes=VMEM_LIMIT),
        name="hgrn_inproj",
    )(x, gain, lb, w_in)


def _head_gate(o, onorm, og):
    return (o * lax.rsqrt(jnp.mean(o * o, axis=-1, keepdims=True) + EPS) * onorm * og).astype(BF16)


def _lower_triangle():
    return (lax.broadcasted_iota(jnp.int32, (CHUNK, CHUNK), 0)
            >= lax.broadcasted_iota(jnp.int32, (CHUNK, CHUNK), 1))


def _cumsum_rows(g):
    n = g.shape[0]
    row = lax.broadcasted_iota(jnp.int32, g.shape, 0)
    s = 1
    while s < n:
        g = g + jnp.where(row >= s, pltpu.roll(g, s, axis=0), 0.0)
        s *= 2
    return g


def _rec_chunk_factored(q, k, big_g, v, st):
    tri = _lower_triangle()
    decay = jnp.exp(big_g)
    qd = (q * decay).astype(BF16)
    kd = (k * jnp.exp(-big_g)).astype(BF16)
    a = lax.dot_general(qd, kd, (((1,), (1,)), ((), ())), preferred_element_type=F32)
    a = jnp.where(tri, a, 0.0).astype(BF16)
    o = _dot(a, v) + lax.dot_general(qd, st.astype(BF16), (((1,), (1,)), ((), ())), preferred_element_type=F32)
    ds = lax.dot_general(v, kd, (((0,), (0,)), ((), ())), preferred_element_type=F32)
    return o, (st + ds) * decay[CHUNK - 1:CHUNK, :]


def _rec_chunk_stepwise(q_ref, k_ref, lg_ref, v, st):
    vt = v.astype(F32).T
    lane = lax.broadcasted_iota(jnp.int32, (1, CHUNK), 1)

    def token_group(grp, carry):
        st, ot = carry
        rows = pl.ds(pl.multiple_of(grp * SUBLANES, SUBLANES), SUBLANES)
        lg, k, q = lg_ref[rows, :], k_ref[rows, :], q_ref[rows, :]
        for i in range(SUBLANES):
            onehot = (lane == grp * SUBLANES + i).astype(F32)
            vcol = jnp.sum(vt * onehot, axis=1, keepdims=True)
            st = st * jnp.exp(lg[i:i + 1, :]) + vcol * k[i:i + 1, :]
            ocol = jnp.sum(st * q[i:i + 1, :], axis=1, keepdims=True)
            ot = ot + ocol * onehot
        return st, ot

    st, ot = lax.fori_loop(0, CHUNK // SUBLANES, token_group, (st, jnp.zeros((HEAD_DIM, CHUNK), F32)))
    return ot.T, st


def _cast_slabs(src_refs, dst_refs):
    for src, dst in zip(src_refs, dst_refs):
        dst[...] = src[...].astype(BF16)


def _slab_specs(weights, n_steps):
    ins, outs, shapes = [], [], []
    for w, layer in weights:
        _, rows, cols = w.shape
        slab = rows // n_steps
        assert slab * n_steps == rows and slab % 16 == 0
        ins.append(pl.BlockSpec((None, slab, cols), lambda s, layer=layer: (layer, jnp.minimum(s, n_steps - 1), 0)))
        outs.append(pl.BlockSpec((slab, cols), lambda s: (jnp.minimum(s, n_steps - 1), 0)))
        shapes.append(jax.ShapeDtypeStruct((rows, cols), BF16))
    return ins, outs, shapes


def _hgrn_mix_kernel(x_ref, gain_ref, lb_ref, w_ref, onorm_ref, wa_ref, wb_ref, wc_ref,
                     gated_ref, s_out_ref, wa_out, wb_out, wc_out,
                     q_s, k_s, lg_s, v_s, og_s, st_ref, slow_ref, *, tm, tiles_per_seq, n_tiles):
    s = pl.program_id(0)
    _cast_slabs((wa_ref, wb_ref, wc_ref), (wa_out, wb_out, wc_out))
    cur, prev = s % 2, (s + 1) % 2
    t_prev = jnp.maximum(s - 1, 0) % tiles_per_seq
    n_chunks = tm // CHUNK
    assert n_chunks == 4

    @pl.when(s == 0)
    def _():
        for ref in (q_s, k_s, lg_s, v_s, og_s):
            ref[1] = jnp.zeros(ref.shape[1:], ref.dtype)
        slow_ref[0] = 0

    @pl.when(t_prev == 0)
    def _():
        st_ref[...] = jnp.zeros_like(st_ref)

    def project(j, u, lb):
        val = _inproj_group(j, u, w_ref, lb)
        if j == 0:
            q_s[cur] = val
        elif j == 1:
            k_s[cur] = 1.0 - val
            lg_s[cur] = jnp.log(val)
        elif j == 2:
            v_s[cur] = val.astype(BF16)
        else:
            og_s[cur] = val

    def recur(c, stepwise):
        rows = slice(c * CHUNK, (c + 1) * CHUNK)
        if not stepwise:
            big_g = _cumsum_rows(lg_s[prev, rows, :])
        for h in range(HEADS):
            cols = slice(h * HEAD_DIM, (h + 1) * HEAD_DIM)
            v = v_s[prev, rows, cols]
            if stepwise:
                o, st = _rec_chunk_stepwise(q_s.at[prev, rows, cols], k_s.at[prev, rows, cols],
                                            lg_s.at[prev, rows, cols], v, st_ref[h])
            else:
                o, st = _rec_chunk_factored(q_s[prev, rows, cols], k_s[prev, rows, cols], big_g[:, cols], v, st_ref[h])
            st_ref[h] = st
            gated_ref[rows, cols] = _head_gate(o, onorm_ref[:, cols], og_s[prev, rows, cols])

    def body(stepwise):
        u = _rms(x_ref[...], gain_ref[...]).astype(BF16)
        lb = _layer_lower_bound(lb_ref)
        for j in range(4):
            project(j, u, lb)
            recur(j, stepwise)
        worst = None
        for c in range(n_chunks):
            tot = jnp.sum(lg_s[cur, c * CHUNK:(c + 1) * CHUNK, :], axis=0, keepdims=True)
            worst = tot if worst is None else jnp.minimum(worst, tot)
        slow_ref[0] = (jnp.min(worst) < -LOG_DECAY_LIMIT).astype(jnp.int32)

    lax.cond(slow_ref[0] == 0, lambda: body(False), lambda: body(True))

    @pl.when(t_prev == tiles_per_seq - 1)
    def _():
        for h in range(HEADS):
            s_out_ref[0, h] = st_ref[h].T


def _hgrn_mix(x, gain, lb, w_in, onorm, later_weights, seqs, tm):
    m = x.shape[0]
    n_tiles = m // tm
    w_ins, w_outs, w_shapes = _slab_specs(later_weights, n_tiles)
    tps = n_tiles // seqs
    cur = lambda s: jnp.minimum(s, n_tiles - 1)
    prev = lambda s: jnp.maximum(s - 1, 0)
    vec = _const_spec((1, D_MODEL))
    buf = lambda dt: pltpu.VMEM((2, tm, D_MODEL), dt)
    return pl.pallas_call(
        functools.partial(_hgrn_mix_kernel, tm=tm, tiles_per_seq=tps, n_tiles=n_tiles),
        grid=(n_tiles + 1,),
        in_specs=[pl.BlockSpec((tm, D_MODEL), lambda s: (cur(s), 0)), vec, _const_spec(lb.shape),
                  _const_spec(w_in.shape), vec] + w_ins,
        out_specs=[pl.BlockSpec((tm, D_MODEL), lambda s: (prev(s), 0)),
                   pl.BlockSpec((1, HEADS, HEAD_DIM, HEAD_DIM), lambda s: (prev(s) // tps, 0, 0, 0))] + w_outs,
        out_shape=[jax.ShapeDtypeStruct((m, D_MODEL), BF16),
                   jax.ShapeDtypeStruct((seqs, HEADS, HEAD_DIM, HEAD_DIM), F32)] + w_shapes,
        scratch_shapes=[buf(F32), buf(F32), buf(F32), buf(BF16), buf(F32),
                        pltpu.VMEM((HEADS, HEAD_DIM, HEAD_DIM), F32), pltpu.SMEM((1,), jnp.int32)],
        compiler_params=pltpu.CompilerParams(dimension_semantics=("arbitrary",), vmem_limit_bytes=VMEM_LIMIT),
        name="hgrn_mix_prompt",
    )(x, gain, lb, w_in, onorm, *[w for w, _ in later_weights])


def _rec_decode_kernel(q_ref, f_ref, v_ref, og_ref, onorm_ref, s_ref, gated_ref, s_out_ref, o_ref):
    n = q_ref.shape[0]
    qt = q_ref[...].T
    ft = f_ref[...].T
    v = v_ref[...].astype(F32)
    for b in range(n):
        vrow = v[b:b + 1, :]
        s_new = vrow + ft[:, b:b + 1] * (s_ref[b, 0] - vrow)
        s_out_ref[b, 0] = s_new
        o_ref[b:b + 1, :] = jnp.sum(qt[:, b:b + 1] * s_new, axis=0, keepdims=True)
    gated_ref[...] = _head_gate(o_ref[...], onorm_ref[...], og_ref[...])


def _rec_decode(q, f, v, og, onorm, s0):
    n = q.shape[0]
    head = pl.BlockSpec((n, HEAD_DIM), lambda h: (0, h))
    state = pl.BlockSpec((n, 1, HEAD_DIM, HEAD_DIM), lambda h: (0, h, 0, 0))
    return pl.pallas_call(
        _rec_decode_kernel,
        grid=(HEADS,),
        in_specs=[head, head, head, head, pl.BlockSpec((1, HEAD_DIM), lambda h: (0, h)), state],
        out_specs=[head, state],
        out_shape=[jax.ShapeDtypeStruct((n, D_MODEL), BF16), jax.ShapeDtypeStruct(s0.shape, F32)],
        scratch_shapes=[pltpu.VMEM((n, HEAD_DIM), F32)],
        compiler_params=pltpu.CompilerParams(dimension_semantics=("parallel",), vmem_limit_bytes=VMEM_LIMIT),
        name="hgrn_recurrence_decode",
    )(q, f, v, og, onorm, s0)


def _mlp_residual(x, hn, wup_ref, wdn_ref):
    acc = x
    for c in range(D_FF // FF_CHUNK):
        cols = slice(c * FF_CHUNK, (c + 1) * FF_CHUNK)
        a = jnp.square(jnp.maximum(_dot(hn, wup_ref[:, cols]), 0.0)).astype(BF16)
        acc = acc + _dot(a, wdn_ref[cols, :])
    return acc


def _hgrn_out_mlp_kernel(gated_ref, x_ref, wout_ref, gain_ref, wup_ref, wdn_ref, *rest):
    n_cast = (len(rest) - 1) // 2
    _cast_slabs(rest[:n_cast], rest[n_cast + 1:])
    out_ref = rest[n_cast]
    x1 = x_ref[...] + _dot(gated_ref[...], wout_ref[...])
    out_ref[...] = _mlp_residual(x1, _rms(x1, gain_ref[...]).astype(BF16), wup_ref, wdn_ref)


def _hgrn_out_mlp(gated, x, w_out, gain, w_up, w_dn, tm, later_weights=()):
    m = x.shape[0]
    row = pl.BlockSpec((tm, D_MODEL), lambda i: (i, 0))
    w_ins, w_outs, w_shapes = _slab_specs(later_weights, m // tm)
    return pl.pallas_call(
        _hgrn_out_mlp_kernel,
        grid=(m // tm,),
        in_specs=[row, row, _const_spec(w_out.shape), _const_spec((1, D_MODEL)), _const_spec(w_up.shape),
                  _const_spec(w_dn.shape)] + w_ins,
        out_specs=[row] + w_outs,
        out_shape=[jax.ShapeDtypeStruct((m, D_MODEL), F32)] + w_shapes,
        compiler_params=pltpu.CompilerParams(dimension_semantics=("arbitrary",), vmem_limit_bytes=VMEM_LIMIT),
        name="hgrn_out_mlp",
    )(gated, x, w_out, gain, w_up, w_dn, *[w for w, _ in later_weights])


def _pool_project(z, wgrp_ref, scale_ref):
    parts = [_dot(z[:, g * POOL_GW:(g + 1) * POOL_GW].astype(BF16), wgrp_ref[g]) for g in range(len(POOL_WINDOWS))]
    return jnp.concatenate(parts, axis=-1) * scale_ref[...]


def _shift_rows(a, k):
    return pltpu.roll(a, k, axis=0)


def _pool_prompt_kernel(x_ref, halo_ref, pgain_ref, wgrp_ref, scale_ref, gain_ref, wup_ref, wdn_ref, fgain_ref,
                        y_ref, hist_ref, x1_ref, hn_ref, pre_ref, *, tm, tiles_per_seq, n_tiles):
    s = pl.program_id(0)
    t = jnp.minimum(s, n_tiles - 1) % tiles_per_seq
    cur, prev = s % 2, (s + 1) % 2
    assert D_FF // FF_CHUNK == len(POOL_WINDOWS) == 4

    def finish():
        y_ref[0] = _rms(pre_ref[...], fgain_ref[...])

    def mixer_and_mlp(with_mlp):
        x = x_ref[0]
        u = _rms(x, pgain_ref[...])
        halo = jnp.where(t > 0, _rms(halo_ref[0], pgain_ref[...]), 0.0)
        ext = jnp.concatenate([halo, u], axis=0)
        pos = t * tm + lax.broadcasted_iota(jnp.int32, (tm, 1), 0)

        def pool_group(g):
            w = POOL_WINDOWS[g]
            cols = slice(g * POOL_GW, (g + 1) * POOL_GW)
            win = ext[:, cols]
            k = 1
            while k < w:
                win = win + _shift_rows(win, k)
                k *= 2
            cnt = jnp.minimum(w, pos + 1).astype(F32)
            z = (win[HALO:, :] / cnt - u[:, cols]).astype(BF16)
            return x[:, cols] + _dot(z, wgrp_ref[g]) * scale_ref[:, cols]

        def mlp_chunk(c, hn, acc):
            ff = slice(c * FF_CHUNK, (c + 1) * FF_CHUNK)
            a = jnp.square(jnp.maximum(_dot(hn, wup_ref[:, ff]), 0.0)).astype(BF16)
            return acc + _dot(a, wdn_ref[ff, :])

        if with_mlp:
            hn_prev = hn_ref[prev]
            acc = mlp_chunk(0, hn_prev, x1_ref[prev])
        parts = [pool_group(0), pool_group(1)]
        if with_mlp:
            acc = mlp_chunk(1, hn_prev, acc)
        parts += [pool_group(2), pool_group(3)]
        if with_mlp:
            acc = mlp_chunk(2, hn_prev, acc)
        x1_new = jnp.concatenate(parts, axis=-1)
        x1_ref[cur] = x1_new
        hn_ref[cur] = _rms(x1_new, gain_ref[...]).astype(BF16)
        if with_mlp:
            pre_ref[...] = mlp_chunk(3, hn_prev, acc)

        @pl.when(t == tiles_per_seq - 1)
        def _():
            hist_ref[0] = ext[HALO + tm - POOL_HIST:, :]

    def head():
        pre_ref[...] = jnp.zeros_like(pre_ref)
        mixer_and_mlp(False)

    def full():
        finish()
        mixer_and_mlp(True)

    lax.cond(s == 0, head, lambda: lax.cond(s == n_tiles + 1, finish, full))


def _pool_prompt(x, pgain, wgrp, scale, gain, w_up, w_dn, fgain, tm):
    b, l, _ = x.shape
    vec = _const_spec((1, D_MODEL))
    tps = l // tm
    n_tiles = b * tps
    blocks_per_tile = tm // HALO
    cur = lambda s: jnp.minimum(s, n_tiles - 1)
    done = lambda s: jnp.maximum(s - 2, 0)
    return pl.pallas_call(
        functools.partial(_pool_prompt_kernel, tm=tm, tiles_per_seq=tps, n_tiles=n_tiles),
        grid=(n_tiles + 2,),
        in_specs=[pl.BlockSpec((1, tm, D_MODEL), lambda s: (cur(s) // tps, cur(s) % tps, 0)),
                  pl.BlockSpec((1, HALO, D_MODEL),
                               lambda s: (cur(s) // tps, jnp.maximum((cur(s) % tps) * blocks_per_tile - 1, 0), 0)),
                  vec, _const_spec(wgrp.shape), vec, vec, _const_spec(w_up.shape), _const_spec(w_dn.shape), vec],
        out_specs=[pl.BlockSpec((1, tm, D_MODEL), lambda s: (done(s) // tps, done(s) % tps, 0)),
                   pl.BlockSpec((1, POOL_HIST, D_MODEL), lambda s: (cur(s) // tps, 0, 0))],
        out_shape=[jax.ShapeDtypeStruct((b, l, D_MODEL), F32), jax.ShapeDtypeStruct((b, POOL_HIST, D_MODEL), F32)],
        scratch_shapes=[pltpu.VMEM((2, tm, D_MODEL), F32), pltpu.VMEM((2, tm, D_MODEL), BF16),
                        pltpu.VMEM((tm, D_MODEL), F32)],
        compiler_params=pltpu.CompilerParams(dimension_semantics=("arbitrary",), vmem_limit_bytes=VMEM_LIMIT),
        name="pool_mlp_prompt",
    )(x, x, pgain, wgrp, scale, gain, w_up, w_dn, fgain)


def _pool_decode_kernel(x_ref, hist_ref, pgain_ref, wgrp_ref, scale_ref, gain_ref, wup_ref, wdn_ref, fgain_ref,
                        y_ref, hist_out_ref):
    x = x_ref[...]
    u = _rms(x, pgain_ref[...])
    parts = []
    for g, w in enumerate(POOL_WINDOWS):
        cols = slice(g * POOL_GW, (g + 1) * POOL_GW)
        acc = u[:, cols]
        for j in range(1, w):
            acc = acc + hist_ref[POOL_HIST - j, :, cols]
        parts.append(acc / float(min(w, PAST_LEN + 1)) - u[:, cols])
    z = jnp.concatenate(parts, axis=-1)
    x1 = x + _pool_project(z, wgrp_ref, scale_ref)
    hn = _rms(x1, gain_ref[...]).astype(BF16)
    y_ref[...] = _rms(_mlp_residual(x1, hn, wup_ref, wdn_ref), fgain_ref[...])
    hist_out_ref[0:POOL_HIST - 1] = hist_ref[1:POOL_HIST]
    hist_out_ref[POOL_HIST - 1] = u


def _pool_decode(x, hist, pgain, wgrp, scale, gain, w_up, w_dn, fgain):
    n = x.shape[0]
    vec = _const_spec((1, D_MODEL))
    return pl.pallas_call(
        _pool_decode_kernel,
        grid=(1,),
        in_specs=[_const_spec(x.shape), _const_spec(hist.shape), vec, _const_spec(wgrp.shape), vec, vec,
                  _const_spec(w_up.shape), _const_spec(w_dn.shape), vec],
        out_specs=[pl.BlockSpec(x.shape, lambda i: (0, 0)), pl.BlockSpec(hist.shape, lambda i: (0, 0, 0))],
        out_shape=[jax.ShapeDtypeStruct(x.shape, F32), jax.ShapeDtypeStruct(hist.shape, F32)],
        compiler_params=pltpu.CompilerParams(dimension_semantics=("arbitrary",), vmem_limit_bytes=VMEM_LIMIT),
        name="pool_mlp_decode",
    )(x, hist, pgain, wgrp, scale, gain, w_up, w_dn, fgain)


def kernel(x_prompt, x_sample, state_hgrn, state_pool, hgrn_norm, hgrn_w_in, hgrn_lb, hgrn_onorm, hgrn_w_out,
           pool_norm, pool_w, pool_scale, mlp_norm, mlp_up, mlp_down, final_norm):
    b, l, _ = x_prompt.shape
    n = x_sample.shape[0]
    vec = lambda a: a.reshape(1, D_MODEL)
    w_in = hgrn_w_in[0].astype(BF16)
    onorm = vec(hgrn_onorm[0])

    xp = x_prompt.reshape(b * l, D_MODEL)
    gated, hgrn_p, w_up0, w_dn0, w_out = _hgrn_mix(xp, vec(hgrn_norm[0]), hgrn_lb, w_in, onorm,
                                                   ((mlp_up, 0), (mlp_down, 0), (hgrn_w_out, 0)), b, 512)
    pool_w_rows = pool_w.reshape(pool_w.shape[0], D_MODEL, POOL_GW)
    x2, w_up1, w_dn1, w_pool = _hgrn_out_mlp(gated, xp, w_out, vec(mlp_norm[0]), w_up0, w_dn0, 512,
                                             ((mlp_up, 1), (mlp_down, 1), (pool_w_rows, 0)))
    w_pool = w_pool.reshape(len(POOL_WINDOWS), POOL_GW, POOL_GW)
    pool_args = (vec(pool_norm[0]), w_pool, vec(pool_scale[0]), vec(mlp_norm[1]), w_up1, w_dn1, vec(final_norm))
    y_prompt, pool_p = _pool_prompt(x2.reshape(b, l, D_MODEL), *pool_args, tm=512)

    xs = x_sample.reshape(n, D_MODEL)
    q, f, v, og = _inproj(xs, vec(hgrn_norm[0]), hgrn_lb, w_in, n)
    gated, hgrn_s = _rec_decode(q, f, v, og, onorm, state_hgrn[0])
    x2, = _hgrn_out_mlp(gated, xs, w_out, vec(mlp_norm[0]), w_up0, w_dn0, n)
    y_sample, pool_s = _pool_decode(x2, jnp.transpose(state_pool[0], (1, 0, 2)), *pool_args)

    return (y_prompt, y_sample.reshape(n, 1, D_MODEL), hgrn_p[None], hgrn_s[None], pool_p[None],
            jnp.transpose(pool_s, (1, 0, 2))[None])
```

```python
import functools

import jax
import jax.numpy as jnp
from jax import lax
from jax.experimental import pallas as pl
from jax.experimental.pallas import tpu as pltpu

D_MODEL = 1024
HEADS = 8
HEAD_DIM = 128
SUBLANES = 8
D_FF = 4 * D_MODEL
POOL_WINDOWS = (2, 4, 8, 16)
POOL_GW = D_MODEL // len(POOL_WINDOWS)
POOL_HIST = max(POOL_WINDOWS) - 1
HALO = 16
EPS = 1e-6
PAST_LEN = 16384

CHUNK = 128
LOG_DECAY_LIMIT = 80.0
FF_CHUNK = 1024
VMEM_LIMIT = 56 * 1024 * 1024

BF16 = jnp.bfloat16
F32 = jnp.float32


def _const_spec(shape):
    nd = len(shape)
    return pl.BlockSpec(shape, lambda *_: (0,) * nd, pipeline_mode=pl.Buffered(1))


def _rms(x, gain):
    return x * lax.rsqrt(jnp.mean(x * x, axis=-1, keepdims=True) + EPS) * gain


def _sigmoid(x):
    return 1.0 / (1.0 + jnp.exp(-x))


def _dot(a, b):
    return jnp.dot(a, b, preferred_element_type=F32)


def _layer_lower_bound(lb_ref):
    lbr = lb_ref[...]
    e = jnp.exp(lbr - jnp.max(lbr, axis=0, keepdims=True))
    return e[0:1, :] / jnp.sum(e, axis=0, keepdims=True)


def _inproj_activation(j, p, lb):
    if j == 0:
        return p * _sigmoid(p) * (HEAD_DIM ** -0.5)
    if j == 1:
        return lb + (1.0 - lb) * _sigmoid(p)
    if j == 2:
        return p
    return _sigmoid(p)


def _inproj_group(j, u, w_ref, lb):
    return _inproj_activation(j, _dot(u, w_ref[:, j * D_MODEL:(j + 1) * D_MODEL]), lb)


def _inproj_decode_kernel(x_ref, gain_ref, lb_ref, w_ref, wb_ref, qt_ref, ft_ref, v_ref, og_ref):
    j = pl.program_id(0)
    wb = w_ref[...].astype(BF16)
    wb_ref[...] = wb
    p = _dot(_rms(x_ref[...], gain_ref[...]).astype(BF16), wb)
    lb = _layer_lower_bound(lb_ref)

    def heads_transposed(val, out_ref):
        for h in range(HEADS):
            out_ref[h] = val[:, h * HEAD_DIM:(h + 1) * HEAD_DIM].T

    @pl.when(j == 0)
    def _():
        heads_transposed(_inproj_activation(0, p, lb), qt_ref)

    @pl.when(j == 1)
    def _():
        heads_transposed(_inproj_activation(1, p, lb), ft_ref)

    @pl.when(j == 2)
    def _():
        v_ref[...] = p

    @pl.when(j == 3)
    def _():
        og_ref[...] = _inproj_activation(3, p, lb)


def _inproj_decode(x, gain, lb, w_in_f32):
    n = x.shape[0]
    vec = _const_spec((1, D_MODEL))
    tok = pl.BlockSpec((n, D_MODEL), lambda j: (0, 0))
    per_head = pl.BlockSpec((HEADS, HEAD_DIM, n), lambda j: (0, 0, 0))
    return pl.pallas_call(
        _inproj_decode_kernel,
        grid=(4,),
        in_specs=[_const_spec(x.shape), vec, _const_spec(lb.shape),
                  pl.BlockSpec((None, D_MODEL, D_MODEL), lambda j: (0, 0, j))],
        out_specs=[pl.BlockSpec((D_MODEL, D_MODEL), lambda j: (0, j)), per_head, per_head, tok, tok],
        out_shape=[jax.ShapeDtypeStruct((D_MODEL, 4 * D_MODEL), BF16),
                   jax.ShapeDtypeStruct((HEADS, HEAD_DIM, n), F32), jax.ShapeDtypeStruct((HEADS, HEAD_DIM, n), F32),
                   jax.ShapeDtypeStruct((n, D_MODEL), F32), jax.ShapeDtypeStruct((n, D_MODEL), F32)],
        compiler_params=pltpu.CompilerParams(dimension_semantics=("arbitrary",), vmem_limit_bytes=VMEM_LIMIT),
        name="hgrn_inproj_decode",
    )(x, gain, lb, w_in_f32)


def _head_gate(o, onorm, og):
    return (o * lax.rsqrt(jnp.mean(o * o, axis=-1, keepdims=True) + EPS) * onorm * og).astype(BF16)


def _lower_triangle():
    return (lax.broadcasted_iota(jnp.int32, (CHUNK, CHUNK), 0)
            >= lax.broadcasted_iota(jnp.int32, (CHUNK, CHUNK), 1))


def _cumsum_rows(g):
    n = g.shape[0]
    row = lax.broadcasted_iota(jnp.int32, g.shape, 0)
    s = 1
    while s < n:
        g = g + jnp.where(row >= s, pltpu.roll(g, s, axis=0), 0.0)
        s *= 2
    return g


def _rec_chunk_factored(q, k, big_g, v, st):
    tri = _lower_triangle()
    decay = jnp.exp(big_g)
    qd = (q * decay).astype(BF16)
    kd = (k * jnp.exp(-big_g)).astype(BF16)
    a = lax.dot_general(qd, kd, (((1,), (1,)), ((), ())), preferred_element_type=F32)
    a = jnp.where(tri, a, 0.0).astype(BF16)
    o = _dot(a, v) + lax.dot_general(qd, st.astype(BF16), (((1,), (1,)), ((), ())), preferred_element_type=F32)
    ds = lax.dot_general(v, kd, (((0,), (0,)), ((), ())), preferred_element_type=F32)
    return o, (st + ds) * decay[CHUNK - 1:CHUNK, :]


def _rec_chunk_stepwise(q_ref, k_ref, lg_ref, v, st):
    vt = v.astype(F32).T
    lane = lax.broadcasted_iota(jnp.int32, (1, CHUNK), 1)

    def token_group(grp, carry):
        st, ot = carry
        rows = pl.ds(pl.multiple_of(grp * SUBLANES, SUBLANES), SUBLANES)
        lg, k, q = lg_ref[rows, :], k_ref[rows, :], q_ref[rows, :]
        for i in range(SUBLANES):
            onehot = (lane == grp * SUBLANES + i).astype(F32)
            vcol = jnp.sum(vt * onehot, axis=1, keepdims=True)
            st = st * jnp.exp(lg[i:i + 1, :]) + vcol * k[i:i + 1, :]
            ocol = jnp.sum(st * q[i:i + 1, :], axis=1, keepdims=True)
            ot = ot + ocol * onehot
        return st, ot

    st, ot = lax.fori_loop(0, CHUNK // SUBLANES, token_group, (st, jnp.zeros((HEAD_DIM, CHUNK), F32)))
    return ot.T, st


def _cast_slabs(src_refs, dst_refs):
    for src, dst in zip(src_refs, dst_refs):
        dst[...] = src[...].astype(BF16)


def _slab_specs(weights, n_steps):
    ins, outs, shapes = [], [], []
    for w, layer in weights:
        _, rows, cols = w.shape
        slab = rows // n_steps
        assert slab * n_steps == rows and slab % 16 == 0
        ins.append(pl.BlockSpec((None, slab, cols), lambda s, layer=layer: (layer, jnp.minimum(s, n_steps - 1), 0)))
        outs.append(pl.BlockSpec((slab, cols), lambda s: (jnp.minimum(s, n_steps - 1), 0)))
        shapes.append(jax.ShapeDtypeStruct((rows, cols), BF16))
    return ins, outs, shapes


def _hgrn_mix_kernel(x_ref, gain_ref, lb_ref, w_ref, onorm_ref, wa_ref, wb_ref, wc_ref,
                     gated_ref, s_out_ref, wa_out, wb_out, wc_out,
                     q_s, k_s, lg_s, v_s, og_s, st_ref, slow_ref, *, tm, tiles_per_seq, n_tiles):
    s = pl.program_id(0)
    _cast_slabs((wa_ref, wb_ref, wc_ref), (wa_out, wb_out, wc_out))
    cur, prev = s % 2, (s + 1) % 2
    t_prev = jnp.maximum(s - 1, 0) % tiles_per_seq
    n_chunks = tm // CHUNK
    assert n_chunks == 4

    @pl.when(s == 0)
    def _():
        for ref in (q_s, k_s, lg_s, v_s, og_s):
            ref[1] = jnp.zeros(ref.shape[1:], ref.dtype)
        slow_ref[0] = 0

    @pl.when(t_prev == 0)
    def _():
        st_ref[...] = jnp.zeros_like(st_ref)

    def project(j, r, u, lb):
        rs = slice(r * (tm // 2), (r + 1) * (tm // 2))
        val = _inproj_group(j, u[rs, :], w_ref, lb)
        if j == 0:
            q_s[cur, rs, :] = val
        elif j == 1:
            k_s[cur, rs, :] = 1.0 - val
            lg_s[cur, rs, :] = jnp.log(val)
        elif j == 2:
            v_s[cur, rs, :] = val.astype(BF16)
        else:
            og_s[cur, rs, :] = val

    def recur(c, heads, big_g, stepwise):
        rows = slice(c * CHUNK, (c + 1) * CHUNK)
        for h in heads:
            cols = slice(h * HEAD_DIM, (h + 1) * HEAD_DIM)
            v = v_s[prev, rows, cols]
            if stepwise:
                o, st = _rec_chunk_stepwise(q_s.at[prev, rows, cols], k_s.at[prev, rows, cols],
                                            lg_s.at[prev, rows, cols], v, st_ref[h])
            else:
                o, st = _rec_chunk_factored(q_s[prev, rows, cols], k_s[prev, rows, cols], big_g[:, cols], v, st_ref[h])
            st_ref[h] = st
            gated_ref[rows, cols] = _head_gate(o, onorm_ref[:, cols], og_s[prev, rows, cols])

    def body(stepwise):
        u = _rms(x_ref[...], gain_ref[...]).astype(BF16)
        lb = _layer_lower_bound(lb_ref)
        for j in range(4):
            big_g = None if stepwise else _cumsum_rows(lg_s[prev, j * CHUNK:(j + 1) * CHUNK, :])
            for r in range(2):
                project(j, r, u, lb)
                recur(j, range(r * HEADS // 2, (r + 1) * HEADS // 2), big_g, stepwise)
        worst = None
        for c in range(n_chunks):
            tot = jnp.sum(lg_s[cur, c * CHUNK:(c + 1) * CHUNK, :], axis=0, keepdims=True)
            worst = tot if worst is None else jnp.minimum(worst, tot)
        slow_ref[0] = (jnp.min(worst) < -LOG_DECAY_LIMIT).astype(jnp.int32)

    lax.cond(slow_ref[0] == 0, lambda: body(False), lambda: body(True))

    @pl.when(t_prev == tiles_per_seq - 1)
    def _():
        for h in range(HEADS):
            s_out_ref[0, h] = st_ref[h].T


def _hgrn_mix(x, gain, lb, w_in, onorm, later_weights, seqs, tm):
    m = x.shape[0]
    n_tiles = m // tm
    w_ins, w_outs, w_shapes = _slab_specs(later_weights, n_tiles)
    tps = n_tiles // seqs
    cur = lambda s: jnp.minimum(s, n_tiles - 1)
    prev = lambda s: jnp.maximum(s - 1, 0)
    vec = _const_spec((1, D_MODEL))
    buf = lambda dt: pltpu.VMEM((2, tm, D_MODEL), dt)
    return pl.pallas_call(
        functools.partial(_hgrn_mix_kernel, tm=tm, tiles_per_seq=tps, n_tiles=n_tiles),
        grid=(n_tiles + 1,),
        in_specs=[pl.BlockSpec((tm, D_MODEL), lambda s: (cur(s), 0)), vec, _const_spec(lb.shape),
                  _const_spec(w_in.shape), vec] + w_ins,
        out_specs=[pl.BlockSpec((tm, D_MODEL), lambda s: (prev(s), 0)),
                   pl.BlockSpec((1, HEADS, HEAD_DIM, HEAD_DIM), lambda s: (prev(s) // tps, 0, 0, 0))] + w_outs,
        out_shape=[jax.ShapeDtypeStruct((m, D_MODEL), BF16),
                   jax.ShapeDtypeStruct((seqs, HEADS, HEAD_DIM, HEAD_DIM), F32)] + w_shapes,
        scratch_shapes=[buf(F32), buf(F32), buf(F32), buf(BF16), buf(F32),
                        pltpu.VMEM((HEADS, HEAD_DIM, HEAD_DIM), F32), pltpu.SMEM((1,), jnp.int32)],
        compiler_params=pltpu.CompilerParams(dimension_semantics=("arbitrary",), vmem_limit_bytes=VMEM_LIMIT),
        name="hgrn_mix_prompt",
    )(x, gain, lb, w_in, onorm, *[w for w, _ in later_weights])


def _mlp_residual(x, hn, wup_ref, wdn_ref):
    acc = x
    for c in range(D_FF // FF_CHUNK):
        cols = slice(c * FF_CHUNK, (c + 1) * FF_CHUNK)
        a = jnp.square(jnp.maximum(_dot(hn, wup_ref[:, cols]), 0.0)).astype(BF16)
        acc = acc + _dot(a, wdn_ref[cols, :])
    return acc


DEC_PER_STEP = 4


def _decode_state_update(step, j, qt_ref, ft_ref, v_ref, s_ref, o_ref, s_out_ref):
    shift = (HEAD_DIM - (DEC_PER_STEP * step + j)) % HEAD_DIM
    for h in range(HEADS):
        cols = slice(h * HEAD_DIM, (h + 1) * HEAD_DIM)
        fcol = pltpu.roll(ft_ref[h], shift, axis=1)[:, 0:1]
        qcol = pltpu.roll(qt_ref[h], shift, axis=1)[:, 0:1]
        vrow = v_ref[j:j + 1, cols]
        s_new = vrow + fcol * (s_ref[j, h] - vrow)
        s_out_ref[j, h] = s_new
        o_ref[j:j + 1, cols] = jnp.sum(qcol * s_new, axis=0, keepdims=True)


def _hgrn_out_mlp_kernel(gated_ref, x_ref, wout_ref, gain_ref, wup_ref, wdn_ref, *rest, n_cast):
    cast_in, side_in = rest[:n_cast], rest[n_cast:n_cast + 4]
    out_ref = rest[n_cast + 4]
    cast_out, side_out = rest[n_cast + 5:2 * n_cast + 5], rest[2 * n_cast + 5:]
    _cast_slabs(cast_in, cast_out)
    x1 = x_ref[...] + _dot(gated_ref[...], wout_ref[...])
    hn = _rms(x1, gain_ref[...]).astype(BF16)
    acc = x1
    assert D_FF // FF_CHUNK == DEC_PER_STEP
    for c in range(D_FF // FF_CHUNK):
        cols = slice(c * FF_CHUNK, (c + 1) * FF_CHUNK)
        a = jnp.square(jnp.maximum(_dot(hn, wup_ref[:, cols]), 0.0)).astype(BF16)
        acc = acc + _dot(a, wdn_ref[cols, :])
        _decode_state_update(pl.program_id(0), c, *side_in, *side_out)
    out_ref[...] = acc


def _hgrn_out_mlp(gated, x, w_out, gain, w_up, w_dn, tm, later_weights, decode):
    m = x.shape[0]
    steps = m // tm
    qt, ft, v, s0 = decode
    assert v.shape == (steps, DEC_PER_STEP, D_MODEL) and s0.shape[0] == steps * DEC_PER_STEP
    row = pl.BlockSpec((tm, D_MODEL), lambda i: (i, 0))
    w_ins, w_outs, w_shapes = _slab_specs(later_weights, steps)
    tok = pl.BlockSpec((None, DEC_PER_STEP, D_MODEL), lambda i: (i, 0, 0))
    state = pl.BlockSpec((DEC_PER_STEP, HEADS, HEAD_DIM, HEAD_DIM), lambda i: (i, 0, 0, 0))
    return pl.pallas_call(
        functools.partial(_hgrn_out_mlp_kernel, n_cast=len(later_weights)),
        grid=(steps,),
        in_specs=[row, row, _const_spec(w_out.shape), _const_spec((1, D_MODEL)), _const_spec(w_up.shape),
                  _const_spec(w_dn.shape)] + w_ins + [_const_spec(qt.shape), _const_spec(ft.shape), tok, state],
        out_specs=[row] + w_outs + [tok, state],
        out_shape=[jax.ShapeDtypeStruct((m, D_MODEL), F32)] + w_shapes
                  + [jax.ShapeDtypeStruct(v.shape, F32), jax.ShapeDtypeStruct(s0.shape, F32)],
        compiler_params=pltpu.CompilerParams(dimension_semantics=("arbitrary",), vmem_limit_bytes=VMEM_LIMIT),
        name="hgrn_out_mlp",
    )(gated, x, w_out, gain, w_up, w_dn, *[w for w, _ in later_weights], qt, ft, v, s0)


def _hgrn_out_mlp_decode_kernel(o_ref, og_ref, onorm_ref, x_ref, wout_ref, gain_ref, wup_ref, wdn_ref, out_ref):
    gated = jnp.concatenate(
        [_head_gate(o_ref[:, h * HEAD_DIM:(h + 1) * HEAD_DIM], onorm_ref[:, h * HEAD_DIM:(h + 1) * HEAD_DIM],
                    og_ref[:, h * HEAD_DIM:(h + 1) * HEAD_DIM]) for h in range(HEADS)], axis=-1)
    x1 = x_ref[...] + _dot(gated, wout_ref[...])
    out_ref[...] = _mlp_residual(x1, _rms(x1, gain_ref[...]).astype(BF16), wup_ref, wdn_ref)


def _hgrn_out_mlp_decode(o, og, onorm, x, w_out, gain, w_up, w_dn):
    args = (o, og, onorm, x, w_out, gain, w_up, w_dn)
    return pl.pallas_call(
        _hgrn_out_mlp_decode_kernel,
        grid=(1,),
        in_specs=[_const_spec(a.shape) for a in args],
        out_specs=pl.BlockSpec(x.shape, lambda i: (0, 0)),
        out_shape=jax.ShapeDtypeStruct(x.shape, F32),
        compiler_params=pltpu.CompilerParams(dimension_semantics=("arbitrary",), vmem_limit_bytes=VMEM_LIMIT),
        name="hgrn_out_mlp_decode",
    )(*args)


def _pool_project(z, wgrp_ref, scale_ref):
    parts = [_dot(z[:, g * POOL_GW:(g + 1) * POOL_GW].astype(BF16), wgrp_ref[g]) for g in range(len(POOL_WINDOWS))]
    return jnp.concatenate(parts, axis=-1) * scale_ref[...]


def _shift_rows(a, k):
    return pltpu.roll(a, k, axis=0)


def _pool_prompt_kernel(x_ref, halo_ref, pgain_ref, wgrp_ref, scale_ref, gain_ref, wup_ref, wdn_ref, fgain_ref,
                        y_ref, hist_ref, x1_ref, hn_ref, pre_ref, *, tm, tiles_per_seq, n_tiles):
    s = pl.program_id(0)
    t = jnp.minimum(s, n_tiles - 1) % tiles_per_seq
    cur, prev = s % 2, (s + 1) % 2
    assert D_FF // FF_CHUNK == len(POOL_WINDOWS) == 4

    def finish():
        y_ref[0] = _rms(pre_ref[...], fgain_ref[...])

    def mixer_and_mlp(with_mlp):
        x = x_ref[0]
        u = _rms(x, pgain_ref[...])
        halo = jnp.where(t > 0, _rms(halo_ref[0], pgain_ref[...]), 0.0)
        ext = jnp.concatenate([halo, u], axis=0)
        pos = t * tm + lax.broadcasted_iota(jnp.int32, (tm, 1), 0)

        def pool_group(g):
            w = POOL_WINDOWS[g]
            cols = slice(g * POOL_GW, (g + 1) * POOL_GW)
            win = ext[:, cols]
            k = 1
            while k < w:
                win = win + _shift_rows(win, k)
                k *= 2
            cnt = jnp.minimum(w, pos + 1).astype(F32)
            z = (win[HALO:, :] / cnt - u[:, cols]).astype(BF16)
            return x[:, cols] + _dot(z, wgrp_ref[g]) * scale_ref[:, cols]

        def mlp_chunk(c, hn, acc):
            ff = slice(c * FF_CHUNK, (c + 1) * FF_CHUNK)
            a = jnp.square(jnp.maximum(_dot(hn, wup_ref[:, ff]), 0.0)).astype(BF16)
            return acc + _dot(a, wdn_ref[ff, :])

        if with_mlp:
            hn_prev = hn_ref[prev]
            acc = mlp_chunk(0, hn_prev, x1_ref[prev])
        parts = [pool_group(0), pool_group(1)]
        if with_mlp:
            acc = mlp_chunk(1, hn_prev, acc)
        parts += [pool_group(2), pool_group(3)]
        if with_mlp:
            acc = mlp_chunk(2, hn_prev, acc)
        x1_new = jnp.concatenate(parts, axis=-1)
        x1_ref[cur] = x1_new
        hn_ref[cur] = _rms(x1_new, gain_ref[...]).astype(BF16)
        if with_mlp:
            pre_ref[...] = mlp_chunk(3, hn_prev, acc)

        @pl.when(t == tiles_per_seq - 1)
        def _():
            hist_ref[0] = ext[HALO + tm - POOL_HIST:, :]

    def head():
        pre_ref[...] = jnp.zeros_like(pre_ref)
        mixer_and_mlp(False)

    def full():
        finish()
        mixer_and_mlp(True)

    lax.cond(s == 0, head, lambda: lax.cond(s == n_tiles + 1, finish, full))


def _pool_prompt(x, pgain, wgrp, scale, gain, w_up, w_dn, fgain, tm):
    b, l, _ = x.shape
    vec = _const_spec((1, D_MODEL))
    tps = l // tm
    n_tiles = b * tps
    blocks_per_tile = tm // HALO
    cur = lambda s: jnp.minimum(s, n_tiles - 1)
    done = lambda s: jnp.maximum(s - 2, 0)
    return pl.pallas_call(
        functools.partial(_pool_prompt_kernel, tm=tm, tiles_per_seq=tps, n_tiles=n_tiles),
        grid=(n_tiles + 2,),
        in_specs=[pl.BlockSpec((1, tm, D_MODEL), lambda s: (cur(s) // tps, cur(s) % tps, 0)),
                  pl.BlockSpec((1, HALO, D_MODEL),
                               lambda s: (cur(s) // tps, jnp.maximum((cur(s) % tps) * blocks_per_tile - 1, 0), 0)),
                  vec, _const_spec(wgrp.shape), vec, vec, _const_spec(w_up.shape), _const_spec(w_dn.shape), vec],
        out_specs=[pl.BlockSpec((1, tm, D_MODEL), lambda s: (done(s) // tps, done(s) % tps, 0)),
                   pl.BlockSpec((1, POOL_HIST, D_MODEL), lambda s: (cur(s) // tps, 0, 0))],
        out_shape=[jax.ShapeDtypeStruct((b, l, D_MODEL), F32), jax.ShapeDtypeStruct((b, POOL_HIST, D_MODEL), F32)],
        scratch_shapes=[pltpu.VMEM((2, tm, D_MODEL), F32), pltpu.VMEM((2, tm, D_MODEL), BF16),
                        pltpu.VMEM((tm, D_MODEL), F32)],
        compiler_params=pltpu.CompilerParams(dimension_semantics=("arbitrary",), vmem_limit_bytes=VMEM_LIMIT),
        name="pool_mlp_prompt",
    )(x, x, pgain, wgrp, scale, gain, w_up, w_dn, fgain)


def _pool_decode_kernel(x_ref, hist_ref, pgain_ref, wgrp_ref, scale_ref, gain_ref, wup_ref, wdn_ref, fgain_ref,
                        y_ref, hist_out_ref):
    x = x_ref[...]
    u = _rms(x, pgain_ref[...])
    parts = []
    for g, w in enumerate(POOL_WINDOWS):
        cols = slice(g * POOL_GW, (g + 1) * POOL_GW)
        acc = u[:, cols]
        for j in range(1, w):
            acc = acc + hist_ref[POOL_HIST - j, :, cols]
        parts.append(acc / float(min(w, PAST_LEN + 1)) - u[:, cols])
    z = jnp.concatenate(parts, axis=-1)
    x1 = x + _pool_project(z, wgrp_ref, scale_ref)
    hn = _rms(x1, gain_ref[...]).astype(BF16)
    y_ref[...] = _rms(_mlp_residual(x1, hn, wup_ref, wdn_ref), fgain_ref[...])
    hist_out_ref[0:POOL_HIST - 1] = hist_ref[1:POOL_HIST]
    hist_out_ref[POOL_HIST - 1] = u


def _pool_decode(x, hist, pgain, wgrp, scale, gain, w_up, w_dn, fgain):
    n = x.shape[0]
    vec = _const_spec((1, D_MODEL))
    return pl.pallas_call(
        _pool_decode_kernel,
        grid=(1,),
        in_specs=[_const_spec(x.shape), _const_spec(hist.shape), vec, _const_spec(wgrp.shape), vec, vec,
                  _const_spec(w_up.shape), _const_spec(w_dn.shape), vec],
        out_specs=[pl.BlockSpec(x.shape, lambda i: (0, 0)), pl.BlockSpec(hist.shape, lambda i: (0, 0, 0))],
        out_shape=[jax.ShapeDtypeStruct(x.shape, F32), jax.ShapeDtypeStruct(hist.shape, F32)],
        compiler_params=pltpu.CompilerParams(dimension_semantics=("arbitrary",), vmem_limit_bytes=VMEM_LIMIT),
        name="pool_mlp_decode",
    )(x, hist, pgain, wgrp, scale, gain, w_up, w_dn, fgain)


def kernel(x_prompt, x_sample, state_hgrn, state_pool, hgrn_norm, hgrn_w_in, hgrn_lb, hgrn_onorm, hgrn_w_out,
           pool_norm, pool_w, pool_scale, mlp_norm, mlp_up, mlp_down, final_norm):
    b, l, _ = x_prompt.shape
    n = x_sample.shape[0]
    vec = lambda a: a.reshape(1, D_MODEL)
    onorm = vec(hgrn_onorm[0])
    xp = x_prompt.reshape(b * l, D_MODEL)
    xs = x_sample.reshape(n, D_MODEL)
    tm = 512
    steps = b * l // tm

    w_in, qt, ft, v, og = _inproj_decode(xs, vec(hgrn_norm[0]), hgrn_lb, hgrn_w_in)

    gated, hgrn_p, w_up0, w_dn0, w_out = _hgrn_mix(xp, vec(hgrn_norm[0]), hgrn_lb, w_in, onorm,
                                                   ((mlp_up, 0), (mlp_down, 0), (hgrn_w_out, 0)), b, tm)
    pool_w_rows = pool_w.reshape(pool_w.shape[0], D_MODEL, POOL_GW)
    x2, w_up1, w_dn1, w_pool, o_dec, hgrn_s = _hgrn_out_mlp(
        gated, xp, w_out, vec(mlp_norm[0]), w_up0, w_dn0, tm, ((mlp_up, 1), (mlp_down, 1), (pool_w_rows, 0)),
        (qt, ft, v.reshape(steps, n // steps, D_MODEL), state_hgrn[0]))
    w_pool = w_pool.reshape(len(POOL_WINDOWS), POOL_GW, POOL_GW)
    pool_args = (vec(pool_norm[0]), w_pool, vec(pool_scale[0]), vec(mlp_norm[1]), w_up1, w_dn1, vec(final_norm))
    y_prompt, pool_p = _pool_prompt(x2.reshape(b, l, D_MODEL), *pool_args, tm=tm)

    x2 = _hgrn_out_mlp_decode(o_dec.reshape(n, D_MODEL), og, onorm, xs, w_out, vec(mlp_norm[0]), w_up0, w_dn0)
    y_sample, pool_s = _pool_decode(x2, jnp.transpose(state_pool[0], (1, 0, 2)), *pool_args)

    return (y_prompt, y_sample.reshape(n, 1, D_MODEL), hgrn_p[None], hgrn_s[None], pool_p[None],
            jnp.transpose(pool_s, (1, 0, 2))[None])
```

```python
import functools

import jax
import jax.numpy as jnp
from jax import lax
from jax.experimental import pallas as pl
from jax.experimental.pallas import tpu as pltpu

D_MODEL = 1024
HEADS = 8
HEAD_DIM = 128
SUBLANES = 8
D_FF = 4 * D_MODEL
POOL_WINDOWS = (2, 4, 8, 16)
POOL_GW = D_MODEL // len(POOL_WINDOWS)
POOL_HIST = max(POOL_WINDOWS) - 1
HALO = 16
EPS = 1e-6
PAST_LEN = 16384

CHUNK = 128
LOG_DECAY_LIMIT = 80.0
FF_CHUNK = 1024
VMEM_LIMIT = 56 * 1024 * 1024

BF16 = jnp.bfloat16
F32 = jnp.float32


def _const_spec(shape):
    nd = len(shape)
    return pl.BlockSpec(shape, lambda *_: (0,) * nd, pipeline_mode=pl.Buffered(1))


def _rms(x, gain):
    return x * lax.rsqrt(jnp.mean(x * x, axis=-1, keepdims=True) + EPS) * gain


def _sigmoid(x):
    return 1.0 / (1.0 + jnp.exp(-x))


def _dot(a, b):
    return jnp.dot(a, b, preferred_element_type=F32)


def _layer_lower_bound(lb_ref):
    lbr = lb_ref[...]
    e = jnp.exp(lbr - jnp.max(lbr, axis=0, keepdims=True))
    return e[0:1, :] / jnp.sum(e, axis=0, keepdims=True)


def _inproj_activation(j, p, lb):
    if j == 0:
        return p * _sigmoid(p) * (HEAD_DIM ** -0.5)
    if j == 1:
        return lb + (1.0 - lb) * _sigmoid(p)
    if j == 2:
        return p
    return _sigmoid(p)


def _inproj_group(j, u, w_ref, lb):
    return _inproj_activation(j, _dot(u, w_ref[:, j * D_MODEL:(j + 1) * D_MODEL]), lb)


def _inproj_decode_kernel(x_ref, gain_ref, lb_ref, w_ref, wb_ref, qt_ref, ft_ref, v_ref, og_ref):
    j = pl.program_id(0)
    wb = w_ref[...].astype(BF16)
    wb_ref[...] = wb
    p = _dot(_rms(x_ref[...], gain_ref[...]).astype(BF16), wb)
    lb = _layer_lower_bound(lb_ref)

    def heads_transposed(val, out_ref):
        for h in range(HEADS):
            out_ref[h] = val[:, h * HEAD_DIM:(h + 1) * HEAD_DIM].T

    @pl.when(j == 0)
    def _():
        heads_transposed(_inproj_activation(0, p, lb), qt_ref)

    @pl.when(j == 1)
    def _():
        heads_transposed(_inproj_activation(1, p, lb), ft_ref)

    @pl.when(j == 2)
    def _():
        v_ref[...] = p

    @pl.when(j == 3)
    def _():
        og_ref[...] = _inproj_activation(3, p, lb)


def _inproj_decode(x, gain, lb, w_in_f32):
    n = x.shape[0]
    vec = _const_spec((1, D_MODEL))
    tok = pl.BlockSpec((n, D_MODEL), lambda j: (0, 0))
    per_head = pl.BlockSpec((HEADS, HEAD_DIM, n), lambda j: (0, 0, 0))
    return pl.pallas_call(
        _inproj_decode_kernel,
        grid=(4,),
        in_specs=[_const_spec(x.shape), vec, _const_spec(lb.shape),
                  pl.BlockSpec((None, D_MODEL, D_MODEL), lambda j: (0, 0, j))],
        out_specs=[pl.BlockSpec((D_MODEL, D_MODEL), lambda j: (0, j)), per_head, per_head, tok, tok],
        out_shape=[jax.ShapeDtypeStruct((D_MODEL, 4 * D_MODEL), BF16),
                   jax.ShapeDtypeStruct((HEADS, HEAD_DIM, n), F32), jax.ShapeDtypeStruct((HEADS, HEAD_DIM, n), F32),
                   jax.ShapeDtypeStruct((n, D_MODEL), F32), jax.ShapeDtypeStruct((n, D_MODEL), F32)],
        compiler_params=pltpu.CompilerParams(dimension_semantics=("arbitrary",), vmem_limit_bytes=VMEM_LIMIT),
        name="hgrn_inproj_decode",
    )(x, gain, lb, w_in_f32)


def _head_gate(o, onorm, og):
    return (o * lax.rsqrt(jnp.mean(o * o, axis=-1, keepdims=True) + EPS) * onorm * og).astype(BF16)


def _lower_triangle():
    return (lax.broadcasted_iota(jnp.int32, (CHUNK, CHUNK), 0)
            >= lax.broadcasted_iota(jnp.int32, (CHUNK, CHUNK), 1))


def _cumsum_rows(g):
    n = g.shape[0]
    row = lax.broadcasted_iota(jnp.int32, g.shape, 0)
    s = 1
    while s < n:
        g = g + jnp.where(row >= s, pltpu.roll(g, s, axis=0), 0.0)
        s *= 2
    return g


def _rec_chunk_factored(q, k, big_g, v, st):
    tri = _lower_triangle()
    decay = jnp.exp(big_g)
    qd = (q * decay).astype(BF16)
    kd = (k * jnp.exp(-big_g)).astype(BF16)
    a = lax.dot_general(qd, kd, (((1,), (1,)), ((), ())), preferred_element_type=F32)
    a = jnp.where(tri, a, 0.0).astype(BF16)
    o = _dot(a, v) + lax.dot_general(qd, st.astype(BF16), (((1,), (1,)), ((), ())), preferred_element_type=F32)
    ds = lax.dot_general(v, kd, (((0,), (0,)), ((), ())), preferred_element_type=F32)
    return o, (st + ds) * decay[CHUNK - 1:CHUNK, :]


def _rec_chunk_stepwise(q_ref, k_ref, lg_ref, v, st):
    vt = v.astype(F32).T
    lane = lax.broadcasted_iota(jnp.int32, (1, CHUNK), 1)

    def token_group(grp, carry):
        st, ot = carry
        rows = pl.ds(pl.multiple_of(grp * SUBLANES, SUBLANES), SUBLANES)
        lg, k, q = lg_ref[rows, :], k_ref[rows, :], q_ref[rows, :]
        for i in range(SUBLANES):
            onehot = (lane == grp * SUBLANES + i).astype(F32)
            vcol = jnp.sum(vt * onehot, axis=1, keepdims=True)
            st = st * jnp.exp(lg[i:i + 1, :]) + vcol * k[i:i + 1, :]
            ocol = jnp.sum(st * q[i:i + 1, :], axis=1, keepdims=True)
            ot = ot + ocol * onehot
        return st, ot

    st, ot = lax.fori_loop(0, CHUNK // SUBLANES, token_group, (st, jnp.zeros((HEAD_DIM, CHUNK), F32)))
    return ot.T, st


def _cast_slabs(src_refs, dst_refs):
    for src, dst in zip(src_refs, dst_refs):
        dst[...] = src[...].astype(BF16)


def _slab_specs(weights, n_steps):
    ins, outs, shapes = [], [], []
    for w, layer in weights:
        _, rows, cols = w.shape
        slab = rows // n_steps
        assert slab * n_steps == rows and slab % 16 == 0
        ins.append(pl.BlockSpec((None, slab, cols), lambda s, layer=layer: (layer, jnp.minimum(s, n_steps - 1), 0)))
        outs.append(pl.BlockSpec((slab, cols), lambda s: (jnp.minimum(s, n_steps - 1), 0)))
        shapes.append(jax.ShapeDtypeStruct((rows, cols), BF16))
    return ins, outs, shapes


def _hgrn_mix_kernel(x_ref, gain_ref, lb_ref, w_ref, onorm_ref, wa_ref, wb_ref, wc_ref,
                     gated_ref, s_out_ref, wa_out, wb_out, wc_out,
                     q_s, k_s, lg_s, v_s, og_s, st_ref, slow_ref, *, tm, tiles_per_seq, n_tiles):
    s = pl.program_id(0)
    _cast_slabs((wa_ref, wb_ref, wc_ref), (wa_out, wb_out, wc_out))
    t_prev = jnp.maximum(s - 1, 0) % tiles_per_seq
    n_chunks = tm // CHUNK
    assert n_chunks == 4

    @pl.when(s == 0)
    def _():
        for ref in (q_s, k_s, lg_s, v_s, og_s):
            ref[1] = jnp.zeros(ref.shape[1:], ref.dtype)
        slow_ref[0] = 0

    @pl.when(t_prev == 0)
    def _():
        st_ref[...] = jnp.zeros_like(st_ref)

    def project(j, r, u, lb, cur):
        rs = slice(r * (tm // 2), (r + 1) * (tm // 2))
        val = _inproj_group(j, u[rs, :], w_ref, lb)
        if j == 0:
            q_s[cur, rs, :] = val
        elif j == 1:
            k_s[cur, rs, :] = 1.0 - val
            lg_s[cur, rs, :] = jnp.log(val)
        elif j == 2:
            v_s[cur, rs, :] = val.astype(BF16)
        else:
            og_s[cur, rs, :] = val

    def recur(c, heads, big_g, stepwise, prev):
        rows = slice(c * CHUNK, (c + 1) * CHUNK)
        for h in heads:
            cols = slice(h * HEAD_DIM, (h + 1) * HEAD_DIM)
            v = v_s[prev, rows, cols]
            if stepwise:
                o, st = _rec_chunk_stepwise(q_s.at[prev, rows, cols], k_s.at[prev, rows, cols],
                                            lg_s.at[prev, rows, cols], v, st_ref[h])
            else:
                o, st = _rec_chunk_factored(q_s[prev, rows, cols], k_s[prev, rows, cols], big_g[:, cols], v, st_ref[h])
            st_ref[h] = st
            gated_ref[rows, cols] = _head_gate(o, onorm_ref[:, cols], og_s[prev, rows, cols])

    def body(stepwise, cur):
        prev = 1 - cur
        u = _rms(x_ref[...], gain_ref[...]).astype(BF16)
        lb = _layer_lower_bound(lb_ref)
        for j in range(4):
            big_g = None if stepwise else _cumsum_rows(lg_s[prev, j * CHUNK:(j + 1) * CHUNK, :])
            for r in range(2):
                project(j, r, u, lb, cur)
                recur(j, range(r * HEADS // 2, (r + 1) * HEADS // 2), big_g, stepwise, prev)
        worst = None
        for c in range(n_chunks):
            tot = jnp.sum(lg_s[cur, c * CHUNK:(c + 1) * CHUNK, :], axis=0, keepdims=True)
            worst = tot if worst is None else jnp.minimum(worst, tot)
        slow_ref[0] = (jnp.min(worst) < -LOG_DECAY_LIMIT).astype(jnp.int32)

    def run(stepwise):
        lax.cond(s % 2 == 0, lambda: body(stepwise, 0), lambda: body(stepwise, 1))

    lax.cond(slow_ref[0] == 0, lambda: run(False), lambda: run(True))

    @pl.when(t_prev == tiles_per_seq - 1)
    def _():
        for h in range(HEADS):
            s_out_ref[0, h] = st_ref[h].T


def _hgrn_mix(x, gain, lb, w_in, onorm, later_weights, seqs, tm):
    m = x.shape[0]
    n_tiles = m // tm
    w_ins, w_outs, w_shapes = _slab_specs(later_weights, n_tiles)
    tps = n_tiles // seqs
    cur = lambda s: jnp.minimum(s, n_tiles - 1)
    prev = lambda s: jnp.maximum(s - 1, 0)
    vec = _const_spec((1, D_MODEL))
    buf = lambda dt: pltpu.VMEM((2, tm, D_MODEL), dt)
    return pl.pallas_call(
        functools.partial(_hgrn_mix_kernel, tm=tm, tiles_per_seq=tps, n_tiles=n_tiles),
        grid=(n_tiles + 1,),
        in_specs=[pl.BlockSpec((tm, D_MODEL), lambda s: (cur(s), 0)), vec, _const_spec(lb.shape),
                  _const_spec(w_in.shape), vec] + w_ins,
        out_specs=[pl.BlockSpec((tm, D_MODEL), lambda s: (prev(s), 0)),
                   pl.BlockSpec((1, HEADS, HEAD_DIM, HEAD_DIM), lambda s: (prev(s) // tps, 0, 0, 0))] + w_outs,
        out_shape=[jax.ShapeDtypeStruct((m, D_MODEL), BF16),
                   jax.ShapeDtypeStruct((seqs, HEADS, HEAD_DIM, HEAD_DIM), F32)] + w_shapes,
        scratch_shapes=[buf(F32), buf(F32), buf(F32), buf(BF16), buf(F32),
                        pltpu.VMEM((HEADS, HEAD_DIM, HEAD_DIM), F32), pltpu.SMEM((1,), jnp.int32)],
        compiler_params=pltpu.CompilerParams(dimension_semantics=("arbitrary",), vmem_limit_bytes=VMEM_LIMIT),
        name="hgrn_mix_prompt",
    )(x, gain, lb, w_in, onorm, *[w for w, _ in later_weights])


def _mlp_residual(x, hn, wup_ref, wdn_ref):
    acc = x
    for c in range(D_FF // FF_CHUNK):
        cols = slice(c * FF_CHUNK, (c + 1) * FF_CHUNK)
        a = jnp.square(jnp.maximum(_dot(hn, wup_ref[:, cols]), 0.0)).astype(BF16)
        acc = acc + _dot(a, wdn_ref[cols, :])
    return acc


DEC_PER_STEP = 4


def _decode_state_update(step, j, qt_ref, ft_ref, v_ref, s_ref, o_ref, s_out_ref):
    shift = (HEAD_DIM - (DEC_PER_STEP * step + j)) % HEAD_DIM
    for h in range(HEADS):
        cols = slice(h * HEAD_DIM, (h + 1) * HEAD_DIM)
        fcol = pltpu.roll(ft_ref[h], shift, axis=1)[:, 0:1]
        qcol = pltpu.roll(qt_ref[h], shift, axis=1)[:, 0:1]
        vrow = v_ref[j:j + 1, cols]
        s_new = vrow + fcol * (s_ref[j, h] - vrow)
        s_out_ref[j, h] = s_new
        o_ref[j:j + 1, cols] = jnp.sum(qcol * s_new, axis=0, keepdims=True)


def _hgrn_out_mlp_kernel(gated_ref, x_ref, wout_ref, gain_ref, wup_ref, wdn_ref, *rest, n_cast):
    cast_in, side_in = rest[:n_cast], rest[n_cast:n_cast + 4]
    out_ref = rest[n_cast + 4]
    cast_out, side_out = rest[n_cast + 5:2 * n_cast + 5], rest[2 * n_cast + 5:]
    _cast_slabs(cast_in, cast_out)
    x1 = x_ref[...] + _dot(gated_ref[...], wout_ref[...])
    hn = _rms(x1, gain_ref[...]).astype(BF16)
    acc = x1
    assert D_FF // FF_CHUNK == DEC_PER_STEP
    for c in range(D_FF // FF_CHUNK):
        cols = slice(c * FF_CHUNK, (c + 1) * FF_CHUNK)
        a = jnp.square(jnp.maximum(_dot(hn, wup_ref[:, cols]), 0.0)).astype(BF16)
        acc = acc + _dot(a, wdn_ref[cols, :])
        _decode_state_update(pl.program_id(0), c, *side_in, *side_out)
    out_ref[...] = acc


def _hgrn_out_mlp(gated, x, w_out, gain, w_up, w_dn, tm, later_weights, decode):
    m = x.shape[0]
    steps = m // tm
    qt, ft, v, s0 = decode
    assert v.shape == (steps, DEC_PER_STEP, D_MODEL) and s0.shape[0] == steps * DEC_PER_STEP
    row = pl.BlockSpec((tm, D_MODEL), lambda i: (i, 0))
    w_ins, w_outs, w_shapes = _slab_specs(later_weights, steps)
    tok = pl.BlockSpec((None, DEC_PER_STEP, D_MODEL), lambda i: (i, 0, 0))
    state = pl.BlockSpec((DEC_PER_STEP, HEADS, HEAD_DIM, HEAD_DIM), lambda i: (i, 0, 0, 0))
    return pl.pallas_call(
        functools.partial(_hgrn_out_mlp_kernel, n_cast=len(later_weights)),
        grid=(steps,),
        in_specs=[row, row, _const_spec(w_out.shape), _const_spec((1, D_MODEL)), _const_spec(w_up.shape),
                  _const_spec(w_dn.shape)] + w_ins + [_const_spec(qt.shape), _const_spec(ft.shape), tok, state],
        out_specs=[row] + w_outs + [tok, state],
        out_shape=[jax.ShapeDtypeStruct((m, D_MODEL), F32)] + w_shapes
                  + [jax.ShapeDtypeStruct(v.shape, F32), jax.ShapeDtypeStruct(s0.shape, F32)],
        compiler_params=pltpu.CompilerParams(dimension_semantics=("arbitrary",), vmem_limit_bytes=VMEM_LIMIT),
        name="hgrn_out_mlp",
    )(gated, x, w_out, gain, w_up, w_dn, *[w for w, _ in later_weights], qt, ft, v, s0)


def _hgrn_out_mlp_decode_kernel(o_ref, og_ref, onorm_ref, x_ref, wout_ref, gain_ref, wup_ref, wdn_ref, out_ref):
    gated = jnp.concatenate(
        [_head_gate(o_ref[:, h * HEAD_DIM:(h + 1) * HEAD_DIM], onorm_ref[:, h * HEAD_DIM:(h + 1) * HEAD_DIM],
                    og_ref[:, h * HEAD_DIM:(h + 1) * HEAD_DIM]) for h in range(HEADS)], axis=-1)
    x1 = x_ref[...] + _dot(gated, wout_ref[...])
    out_ref[...] = _mlp_residual(x1, _rms(x1, gain_ref[...]).astype(BF16), wup_ref, wdn_ref)


def _hgrn_out_mlp_decode(o, og, onorm, x, w_out, gain, w_up, w_dn):
    args = (o, og, onorm, x, w_out, gain, w_up, w_dn)
    return pl.pallas_call(
        _hgrn_out_mlp_decode_kernel,
        grid=(1,),
        in_specs=[_const_spec(a.shape) for a in args],
        out_specs=pl.BlockSpec(x.shape, lambda i: (0, 0)),
        out_shape=jax.ShapeDtypeStruct(x.shape, F32),
        compiler_params=pltpu.CompilerParams(dimension_semantics=("arbitrary",), vmem_limit_bytes=VMEM_LIMIT),
        name="hgrn_out_mlp_decode",
    )(*args)


def _pool_project(z, wgrp_ref, scale_ref):
    parts = [_dot(z[:, g * POOL_GW:(g + 1) * POOL_GW].astype(BF16), wgrp_ref[g]) for g in range(len(POOL_WINDOWS))]
    return jnp.concatenate(parts, axis=-1) * scale_ref[...]


def _shift_rows(a, k):
    return pltpu.roll(a, k, axis=0)


def _pool_prompt_kernel(x_ref, halo_ref, pgain_ref, wgrp_ref, scale_ref, gain_ref, wup_ref, wdn_ref, fgain_ref,
                        y_ref, hist_ref, x1_ref, hn_ref, pre_ref, *, tm, tiles_per_seq, n_tiles):
    s = pl.program_id(0)
    t = jnp.minimum(s, n_tiles - 1) % tiles_per_seq
    cur, prev = s % 2, (s + 1) % 2
    assert D_FF // FF_CHUNK == len(POOL_WINDOWS) == 4

    def finish():
        y_ref[0] = _rms(pre_ref[...], fgain_ref[...])

    def mixer_and_mlp(with_mlp):
        x = x_ref[0]
        u = _rms(x, pgain_ref[...])
        halo = jnp.where(t > 0, _rms(halo_ref[0], pgain_ref[...]), 0.0)
        ext = jnp.concatenate([halo, u], axis=0)
        pos = t * tm + lax.broadcasted_iota(jnp.int32, (tm, 1), 0)

        def pool_group(g):
            w = POOL_WINDOWS[g]
            cols = slice(g * POOL_GW, (g + 1) * POOL_GW)
            win = ext[:, cols]
            k = 1
            while k < w:
                win = win + _shift_rows(win, k)
                k *= 2
            cnt = jnp.minimum(w, pos + 1).astype(F32)
            z = (win[HALO:, :] / cnt - u[:, cols]).astype(BF16)
            return x[:, cols] + _dot(z, wgrp_ref[g]) * scale_ref[:, cols]

        def mlp_chunk(c, hn, acc):
            ff = slice(c * FF_CHUNK, (c + 1) * FF_CHUNK)
            a = jnp.square(jnp.maximum(_dot(hn, wup_ref[:, ff]), 0.0)).astype(BF16)
            return acc + _dot(a, wdn_ref[ff, :])

        if with_mlp:
            hn_prev = hn_ref[prev]
            acc = mlp_chunk(0, hn_prev, x1_ref[prev])
        parts = [pool_group(0), pool_group(1)]
        if with_mlp:
            acc = mlp_chunk(1, hn_prev, acc)
        parts += [pool_group(2), pool_group(3)]
        if with_mlp:
            acc = mlp_chunk(2, hn_prev, acc)
        x1_new = jnp.concatenate(parts, axis=-1)
        x1_ref[cur] = x1_new
        hn_ref[cur] = _rms(x1_new, gain_ref[...]).astype(BF16)
        if with_mlp:
            pre_ref[...] = mlp_chunk(3, hn_prev, acc)

        @pl.when(t == tiles_per_seq - 1)
        def _():
            hist_ref[0] = ext[HALO + tm - POOL_HIST:, :]

    def head():
        pre_ref[...] = jnp.zeros_like(pre_ref)
        mixer_and_mlp(False)

    def full():
        finish()
        mixer_and_mlp(True)

    lax.cond(s == 0, head, lambda: lax.cond(s == n_tiles + 1, finish, full))


def _pool_prompt(x, pgain, wgrp, scale, gain, w_up, w_dn, fgain, tm):
    b, l, _ = x.shape
    vec = _const_spec((1, D_MODEL))
    tps = l // tm
    n_tiles = b * tps
    blocks_per_tile = tm // HALO
    cur = lambda s: jnp.minimum(s, n_tiles - 1)
    done = lambda s: jnp.maximum(s - 2, 0)
    return pl.pallas_call(
        functools.partial(_pool_prompt_kernel, tm=tm, tiles_per_seq=tps, n_tiles=n_tiles),
        grid=(n_tiles + 2,),
        in_specs=[pl.BlockSpec((1, tm, D_MODEL), lambda s: (cur(s) // tps, cur(s) % tps, 0)),
                  pl.BlockSpec((1, HALO, D_MODEL),
                               lambda s: (cur(s) // tps, jnp.maximum((cur(s) % tps) * blocks_per_tile - 1, 0), 0)),
                  vec, _const_spec(wgrp.shape), vec, vec, _const_spec(w_up.shape), _const_spec(w_dn.shape), vec],
        out_specs=[pl.BlockSpec((1, tm, D_MODEL), lambda s: (done(s) // tps, done(s) % tps, 0)),
                   pl.BlockSpec((1, POOL_HIST, D_MODEL), lambda s: (cur(s) // tps, 0, 0))],
        out_shape=[jax.ShapeDtypeStruct((b, l, D_MODEL), F32), jax.ShapeDtypeStruct((b, POOL_HIST, D_MODEL), F32)],
        scratch_shapes=[pltpu.VMEM((2, tm, D_MODEL), F32), pltpu.VMEM((2, tm, D_MODEL), BF16),
                        pltpu.VMEM((tm, D_MODEL), F32)],
        compiler_params=pltpu.CompilerParams(dimension_semantics=("arbitrary",), vmem_limit_bytes=VMEM_LIMIT),
        name="pool_mlp_prompt",
    )(x, x, pgain, wgrp, scale, gain, w_up, w_dn, fgain)


def _pool_decode_kernel(x_ref, hist_ref, pgain_ref, wgrp_ref, scale_ref, gain_ref, wup_ref, wdn_ref, fgain_ref,
                        y_ref, hist_out_ref):
    x = x_ref[...]
    u = _rms(x, pgain_ref[...])
    parts = []
    for g, w in enumerate(POOL_WINDOWS):
        cols = slice(g * POOL_GW, (g + 1) * POOL_GW)
        acc = u[:, cols]
        for j in range(1, w):
            acc = acc + hist_ref[POOL_HIST - j, :, cols]
        parts.append(acc / float(min(w, PAST_LEN + 1)) - u[:, cols])
    z = jnp.concatenate(parts, axis=-1)
    x1 = x + _pool_project(z, wgrp_ref, scale_ref)
    hn = _rms(x1, gain_ref[...]).astype(BF16)
    y_ref[...] = _rms(_mlp_residual(x1, hn, wup_ref, wdn_ref), fgain_ref[...])
    hist_out_ref[0:POOL_HIST - 1] = hist_ref[1:POOL_HIST]
    hist_out_ref[POOL_HIST - 1] = u


def _pool_decode(x, hist, pgain, wgrp, scale, gain, w_up, w_dn, fgain):
    n = x.shape[0]
    vec = _const_spec((1, D_MODEL))
    return pl.pallas_call(
        _pool_decode_kernel,
        grid=(1,),
        in_specs=[_const_spec(x.shape), _const_spec(hist.shape), vec, _const_spec(wgrp.shape), vec, vec,
                  _const_spec(w_up.shape), _const_spec(w_dn.shape), vec],
        out_specs=[pl.BlockSpec(x.shape, lambda i: (0, 0)), pl.BlockSpec(hist.shape, lambda i: (0, 0, 0))],
        out_shape=[jax.ShapeDtypeStruct(x.shape, F32), jax.ShapeDtypeStruct(hist.shape, F32)],
        compiler_params=pltpu.CompilerParams(dimension_semantics=("arbitrary",), vmem_limit_bytes=VMEM_LIMIT),
        name="pool_mlp_decode",
    )(x, hist, pgain, wgrp, scale, gain, w_up, w_dn, fgain)


def kernel(x_prompt, x_sample, state_hgrn, state_pool, hgrn_norm, hgrn_w_in, hgrn_lb, hgrn_onorm, hgrn_w_out,
           pool_norm, pool_w, pool_scale, mlp_norm, mlp_up, mlp_down, final_norm):
    b, l, _ = x_prompt.shape
    n = x_sample.shape[0]
    vec = lambda a: a.reshape(1, D_MODEL)
    onorm = vec(hgrn_onorm[0])
    xp = x_prompt.reshape(b * l, D_MODEL)
    xs = x_sample.reshape(n, D_MODEL)
    tm = 512
    steps = b * l // tm

    w_in, qt, ft, v, og = _inproj_decode(xs, vec(hgrn_norm[0]), hgrn_lb, hgrn_w_in)

    gated, hgrn_p, w_up0, w_dn0, w_out = _hgrn_mix(xp, vec(hgrn_norm[0]), hgrn_lb, w_in, onorm,
                                                   ((mlp_up, 0), (mlp_down, 0), (hgrn_w_out, 0)), b, tm)
    pool_w_rows = pool_w.reshape(pool_w.shape[0], D_MODEL, POOL_GW)
    x2, w_up1, w_dn1, w_pool, o_dec, hgrn_s = _hgrn_out_mlp(
        gated, xp, w_out, vec(mlp_norm[0]), w_up0, w_dn0, tm, ((mlp_up, 1), (mlp_down, 1), (pool_w_rows, 0)),
        (qt, ft, v.reshape(steps, n // steps, D_MODEL), state_hgrn[0]))
    w_pool = w_pool.reshape(len(POOL_WINDOWS), POOL_GW, POOL_GW)
    pool_args = (vec(pool_norm[0]), w_pool, vec(pool_scale[0]), vec(mlp_norm[1]), w_up1, w_dn1, vec(final_norm))
    y_prompt, pool_p = _pool_prompt(x2.reshape(b, l, D_MODEL), *pool_args, tm=tm)

    x2 = _hgrn_out_mlp_decode(o_dec.reshape(n, D_MODEL), og, onorm, xs, w_out, vec(mlp_norm[0]), w_up0, w_dn0)
    y_sample, pool_s = _pool_decode(x2, jnp.transpose(state_pool[0], (1, 0, 2)), *pool_args)

    return (y_prompt, y_sample.reshape(n, 1, D_MODEL), hgrn_p[None], hgrn_s[None], pool_p[None],
            jnp.transpose(pool_s, (1, 0, 2))[None])
```

```python
import functools

import jax
import jax.numpy as jnp
from jax import lax
from jax.experimental import pallas as pl
from jax.experimental.pallas import tpu as pltpu

D_MODEL = 1024
HEADS = 8
HEAD_DIM = 128
SUBLANES = 8
D_FF = 4 * D_MODEL
POOL_WINDOWS = (2, 4, 8, 16)
POOL_GW = D_MODEL // len(POOL_WINDOWS)
POOL_HIST = max(POOL_WINDOWS) - 1
HALO = 16
EPS = 1e-6
PAST_LEN = 16384

CHUNK = 128
LOG_DECAY_LIMIT = 80.0
FF_CHUNK = 1024
VMEM_LIMIT = 56 * 1024 * 1024

BF16 = jnp.bfloat16
F32 = jnp.float32


def _const_spec(shape):
    nd = len(shape)
    return pl.BlockSpec(shape, lambda *_: (0,) * nd, pipeline_mode=pl.Buffered(1))


def _rms(x, gain):
    return x * lax.rsqrt(jnp.mean(x * x, axis=-1, keepdims=True) + EPS) * gain


def _sigmoid(x):
    return 1.0 / (1.0 + jnp.exp(-x))


def _dot(a, b):
    return jnp.dot(a, b, preferred_element_type=F32)


def _layer_lower_bound(lb_ref):
    lbr = lb_ref[...]
    e = jnp.exp(lbr - jnp.max(lbr, axis=0, keepdims=True))
    return e[0:1, :] / jnp.sum(e, axis=0, keepdims=True)


def _inproj_activation(j, p, lb):
    if j == 0:
        return p * _sigmoid(p) * (HEAD_DIM ** -0.5)
    if j == 1:
        return lb + (1.0 - lb) * _sigmoid(p)
    if j == 2:
        return p
    return _sigmoid(p)


def _inproj_group(j, u, w_ref, lb):
    return _inproj_activation(j, _dot(u, w_ref[:, j * D_MODEL:(j + 1) * D_MODEL]), lb)


def _inproj_decode_kernel(x_ref, gain_ref, lb_ref, w_ref, wb_ref, qt_ref, ft_ref, v_ref, og_ref):
    j = pl.program_id(0)
    wb = w_ref[...].astype(BF16)
    wb_ref[...] = wb
    p = _dot(_rms(x_ref[...], gain_ref[...]).astype(BF16), wb)
    lb = _layer_lower_bound(lb_ref)

    def heads_transposed(val, out_ref):
        for h in range(HEADS):
            out_ref[h] = val[:, h * HEAD_DIM:(h + 1) * HEAD_DIM].T

    @pl.when(j == 0)
    def _():
        heads_transposed(_inproj_activation(0, p, lb), qt_ref)

    @pl.when(j == 1)
    def _():
        heads_transposed(_inproj_activation(1, p, lb), ft_ref)

    @pl.when(j == 2)
    def _():
        v_ref[...] = p

    @pl.when(j == 3)
    def _():
        og_ref[...] = _inproj_activation(3, p, lb)


def _inproj_decode(x, gain, lb, w_in_f32):
    n = x.shape[0]
    vec = _const_spec((1, D_MODEL))
    tok = pl.BlockSpec((n, D_MODEL), lambda j: (0, 0))
    per_head = pl.BlockSpec((HEADS, HEAD_DIM, n), lambda j: (0, 0, 0))
    return pl.pallas_call(
        _inproj_decode_kernel,
        grid=(4,),
        in_specs=[_const_spec(x.shape), vec, _const_spec(lb.shape),
                  pl.BlockSpec((None, D_MODEL, D_MODEL), lambda j: (0, 0, j))],
        out_specs=[pl.BlockSpec((D_MODEL, D_MODEL), lambda j: (0, j)), per_head, per_head, tok, tok],
        out_shape=[jax.ShapeDtypeStruct((D_MODEL, 4 * D_MODEL), BF16),
                   jax.ShapeDtypeStruct((HEADS, HEAD_DIM, n), F32), jax.ShapeDtypeStruct((HEADS, HEAD_DIM, n), F32),
                   jax.ShapeDtypeStruct((n, D_MODEL), F32), jax.ShapeDtypeStruct((n, D_MODEL), F32)],
        compiler_params=pltpu.CompilerParams(dimension_semantics=("arbitrary",), vmem_limit_bytes=VMEM_LIMIT),
        name="hgrn_inproj_decode",
    )(x, gain, lb, w_in_f32)


def _head_gate(o, onorm, og):
    return (o * lax.rsqrt(jnp.mean(o * o, axis=-1, keepdims=True) + EPS) * onorm * og).astype(BF16)


def _lower_triangle():
    return (lax.broadcasted_iota(jnp.int32, (CHUNK, CHUNK), 0)
            >= lax.broadcasted_iota(jnp.int32, (CHUNK, CHUNK), 1))


def _cumsum_rows(g):
    n = g.shape[0]
    row = lax.broadcasted_iota(jnp.int32, g.shape, 0)
    s = 1
    while s < n:
        g = g + jnp.where(row >= s, pltpu.roll(g, s, axis=0), 0.0)
        s *= 2
    return g


def _rec_chunk_factored(q, k, big_g, v, st):
    tri = _lower_triangle()
    decay = jnp.exp(big_g)
    qd = (q * decay).astype(BF16)
    kd = (k * jnp.exp(-big_g)).astype(BF16)
    a = lax.dot_general(qd, kd, (((1,), (1,)), ((), ())), preferred_element_type=F32)
    a = jnp.where(tri, a, 0.0).astype(BF16)
    o = _dot(a, v) + lax.dot_general(qd, st.astype(BF16), (((1,), (1,)), ((), ())), preferred_element_type=F32)
    ds = lax.dot_general(v, kd, (((0,), (0,)), ((), ())), preferred_element_type=F32)
    return o, (st + ds) * decay[CHUNK - 1:CHUNK, :]


def _rec_chunk_stepwise(q_ref, k_ref, lg_ref, v, st):
    vt = v.astype(F32).T
    lane = lax.broadcasted_iota(jnp.int32, (1, CHUNK), 1)

    def token_group(grp, carry):
        st, ot = carry
        rows = pl.ds(pl.multiple_of(grp * SUBLANES, SUBLANES), SUBLANES)
        lg, k, q = lg_ref[rows, :], k_ref[rows, :], q_ref[rows, :]
        for i in range(SUBLANES):
            onehot = (lane == grp * SUBLANES + i).astype(F32)
            vcol = jnp.sum(vt * onehot, axis=1, keepdims=True)
            st = st * jnp.exp(lg[i:i + 1, :]) + vcol * k[i:i + 1, :]
            ocol = jnp.sum(st * q[i:i + 1, :], axis=1, keepdims=True)
            ot = ot + ocol * onehot
        return st, ot

    st, ot = lax.fori_loop(0, CHUNK // SUBLANES, token_group, (st, jnp.zeros((HEAD_DIM, CHUNK), F32)))
    return ot.T, st


def _cast_slabs(src_refs, dst_refs):
    for src, dst in zip(src_refs, dst_refs):
        dst[...] = src[...].astype(BF16)


def _slab_specs(weights, n_steps):
    ins, outs, shapes = [], [], []
    for w, layer in weights:
        _, rows, cols = w.shape
        slab = rows // n_steps
        assert slab * n_steps == rows and slab % 16 == 0
        ins.append(pl.BlockSpec((None, slab, cols), lambda s, layer=layer: (layer, jnp.minimum(s, n_steps - 1), 0)))
        outs.append(pl.BlockSpec((slab, cols), lambda s: (jnp.minimum(s, n_steps - 1), 0)))
        shapes.append(jax.ShapeDtypeStruct((rows, cols), BF16))
    return ins, outs, shapes


def _hgrn_mix_kernel(x_ref, gain_ref, lb_ref, w_ref, onorm_ref, wa_ref, wb_ref, wc_ref,
                     gated_ref, s_out_ref, wa_out, wb_out, wc_out,
                     q_s, k_s, lg_s, v_s, og_s, st_ref, slow_ref, *, tm, tiles_per_seq, n_tiles):
    s = pl.program_id(0)
    _cast_slabs((wa_ref, wb_ref, wc_ref), (wa_out, wb_out, wc_out))
    cur, prev = s % 2, (s + 1) % 2
    t_prev = jnp.maximum(s - 1, 0) % tiles_per_seq
    n_chunks = tm // CHUNK
    assert n_chunks == 4

    @pl.when(s == 0)
    def _():
        for ref in (q_s, k_s, lg_s, v_s, og_s):
            ref[1] = jnp.zeros(ref.shape[1:], ref.dtype)
        slow_ref[0] = 0

    @pl.when(t_prev == 0)
    def _():
        st_ref[...] = jnp.zeros_like(st_ref)

    def project(j, r, u, lb):
        rs = slice(r * (tm // 2), (r + 1) * (tm // 2))
        val = _inproj_group(j, u[rs, :], w_ref, lb)
        if j == 0:
            q_s[cur, rs, :] = val
        elif j == 1:
            k_s[cur, rs, :] = 1.0 - val
            lg_s[cur, rs, :] = jnp.log(val)
        elif j == 2:
            v_s[cur, rs, :] = val.astype(BF16)
        else:
            og_s[cur, rs, :] = val

    def recur(rows, heads, big_g):
        for h in heads:
            cols = slice(h * HEAD_DIM, (h + 1) * HEAD_DIM)
            v = v_s[prev, rows, cols]
            if big_g is None:
                o, st = _rec_chunk_stepwise(q_s.at[prev, rows, cols], k_s.at[prev, rows, cols],
                                            lg_s.at[prev, rows, cols], v, st_ref[h])
            else:
                o, st = _rec_chunk_factored(q_s[prev, rows, cols], k_s[prev, rows, cols], big_g[:, cols], v, st_ref[h])
            st_ref[h] = st
            gated_ref[rows, cols] = _head_gate(o, onorm_ref[:, cols], og_s[prev, rows, cols])

    def body(stepwise):
        u = _rms(x_ref[...], gain_ref[...]).astype(BF16)
        lb = _layer_lower_bound(lb_ref)
        for j in range(4):
            rows = slice(j * CHUNK, (j + 1) * CHUNK)
            big_g = None if stepwise else _cumsum_rows(lg_s[prev, rows, :])
            for r in range(2):
                project(j, r, u, lb)
                if not stepwise:
                    recur(rows, range(r * HEADS // 2, (r + 1) * HEADS // 2), big_g)
        if stepwise:

            def chunk(c, carry):
                recur(pl.ds(pl.multiple_of(c * CHUNK, CHUNK), CHUNK), range(HEADS), None)
                return carry

            lax.fori_loop(0, n_chunks, chunk, 0)
        worst = None
        for c in range(n_chunks):
            tot = jnp.sum(lg_s[cur, c * CHUNK:(c + 1) * CHUNK, :], axis=0, keepdims=True)
            worst = tot if worst is None else jnp.minimum(worst, tot)
        slow_ref[0] = (jnp.min(worst) < -LOG_DECAY_LIMIT).astype(jnp.int32)

        @pl.when(t_prev == tiles_per_seq - 1)
        def _():
            for h in range(HEADS):
                s_out_ref[0, h] = st_ref[h].T

    lax.cond(slow_ref[0] == 0, lambda: body(False), lambda: body(True))


def _hgrn_mix(x, gain, lb, w_in, onorm, later_weights, seqs, tm):
    m = x.shape[0]
    n_tiles = m // tm
    w_ins, w_outs, w_shapes = _slab_specs(later_weights, n_tiles)
    tps = n_tiles // seqs
    cur = lambda s: jnp.minimum(s, n_tiles - 1)
    prev = lambda s: jnp.maximum(s - 1, 0)
    vec = _const_spec((1, D_MODEL))
    buf = lambda dt: pltpu.VMEM((2, tm, D_MODEL), dt)
    return pl.pallas_call(
        functools.partial(_hgrn_mix_kernel, tm=tm, tiles_per_seq=tps, n_tiles=n_tiles),
        grid=(n_tiles + 1,),
        in_specs=[pl.BlockSpec((tm, D_MODEL), lambda s: (cur(s), 0)), vec, _const_spec(lb.shape),
                  _const_spec(w_in.shape), vec] + w_ins,
        out_specs=[pl.BlockSpec((tm, D_MODEL), lambda s: (prev(s), 0)),
                   pl.BlockSpec((1, HEADS, HEAD_DIM, HEAD_DIM), lambda s: (prev(s) // tps, 0, 0, 0))] + w_outs,
        out_shape=[jax.ShapeDtypeStruct((m, D_MODEL), BF16),
                   jax.ShapeDtypeStruct((seqs, HEADS, HEAD_DIM, HEAD_DIM), F32)] + w_shapes,
        scratch_shapes=[buf(F32), buf(F32), buf(F32), buf(BF16), buf(F32),
                        pltpu.VMEM((HEADS, HEAD_DIM, HEAD_DIM), F32), pltpu.SMEM((1,), jnp.int32)],
        compiler_params=pltpu.CompilerParams(dimension_semantics=("arbitrary",), vmem_limit_bytes=VMEM_LIMIT),
        name="hgrn_mix_prompt",
    )(x, gain, lb, w_in, onorm, *[w for w, _ in later_weights])


def _mlp_residual(x, hn, wup_ref, wdn_ref):
    acc = x
    for c in range(D_FF // FF_CHUNK):
        cols = slice(c * FF_CHUNK, (c + 1) * FF_CHUNK)
        a = jnp.square(jnp.maximum(_dot(hn, wup_ref[:, cols]), 0.0)).astype(BF16)
        acc = acc + _dot(a, wdn_ref[cols, :])
    return acc


DEC_PER_STEP = 4


def _decode_state_update(step, j, qt_ref, ft_ref, v_ref, s_ref, o_ref, s_out_ref):
    shift = (HEAD_DIM - (DEC_PER_STEP * step + j)) % HEAD_DIM
    for h in range(HEADS):
        cols = slice(h * HEAD_DIM, (h + 1) * HEAD_DIM)
        fcol = pltpu.roll(ft_ref[h], shift, axis=1)[:, 0:1]
        qcol = pltpu.roll(qt_ref[h], shift, axis=1)[:, 0:1]
        vrow = v_ref[j:j + 1, cols]
        s_new = vrow + fcol * (s_ref[j, h] - vrow)
        s_out_ref[j, h] = s_new
        o_ref[j:j + 1, cols] = jnp.sum(qcol * s_new, axis=0, keepdims=True)


def _hgrn_out_mlp_kernel(gated_ref, x_ref, wout_ref, gain_ref, wup_ref, wdn_ref, *rest, n_cast):
    cast_in, side_in = rest[:n_cast], rest[n_cast:n_cast + 4]
    out_ref = rest[n_cast + 4]
    cast_out, side_out = rest[n_cast + 5:2 * n_cast + 5], rest[2 * n_cast + 5:]
    _cast_slabs(cast_in, cast_out)
    x1 = x_ref[...] + _dot(gated_ref[...], wout_ref[...])
    hn = _rms(x1, gain_ref[...]).astype(BF16)
    acc = x1
    assert D_FF // FF_CHUNK == DEC_PER_STEP
    for c in range(D_FF // FF_CHUNK):
        cols = slice(c * FF_CHUNK, (c + 1) * FF_CHUNK)
        a = jnp.square(jnp.maximum(_dot(hn, wup_ref[:, cols]), 0.0)).astype(BF16)
        acc = acc + _dot(a, wdn_ref[cols, :])
        _decode_state_update(pl.program_id(0), c, *side_in, *side_out)
    out_ref[...] = acc


def _hgrn_out_mlp(gated, x, w_out, gain, w_up, w_dn, tm, later_weights, decode):
    m = x.shape[0]
    steps = m // tm
    qt, ft, v, s0 = decode
    assert v.shape == (steps, DEC_PER_STEP, D_MODEL) and s0.shape[0] == steps * DEC_PER_STEP
    row = pl.BlockSpec((tm, D_MODEL), lambda i: (i, 0))
    w_ins, w_outs, w_shapes = _slab_specs(later_weights, steps)
    tok = pl.BlockSpec((None, DEC_PER_STEP, D_MODEL), lambda i: (i, 0, 0))
    state = pl.BlockSpec((DEC_PER_STEP, HEADS, HEAD_DIM, HEAD_DIM), lambda i: (i, 0, 0, 0))
    return pl.pallas_call(
        functools.partial(_hgrn_out_mlp_kernel, n_cast=len(later_weights)),
        grid=(steps,),
        in_specs=[row, row, _const_spec(w_out.shape), _const_spec((1, D_MODEL)), _const_spec(w_up.shape),
                  _const_spec(w_dn.shape)] + w_ins + [_const_spec(qt.shape), _const_spec(ft.shape), tok, state],
        out_specs=[row] + w_outs + [tok, state],
        out_shape=[jax.ShapeDtypeStruct((m, D_MODEL), F32)] + w_shapes
                  + [jax.ShapeDtypeStruct(v.shape, F32), jax.ShapeDtypeStruct(s0.shape, F32)],
        compiler_params=pltpu.CompilerParams(dimension_semantics=("arbitrary",), vmem_limit_bytes=VMEM_LIMIT),
        name="hgrn_out_mlp",
    )(gated, x, w_out, gain, w_up, w_dn, *[w for w, _ in later_weights], qt, ft, v, s0)


def _hgrn_out_mlp_decode_kernel(o_ref, og_ref, onorm_ref, x_ref, wout_ref, gain_ref, wup_ref, wdn_ref, out_ref):
    gated = jnp.concatenate(
        [_head_gate(o_ref[:, h * HEAD_DIM:(h + 1) * HEAD_DIM], onorm_ref[:, h * HEAD_DIM:(h + 1) * HEAD_DIM],
                    og_ref[:, h * HEAD_DIM:(h + 1) * HEAD_DIM]) for h in range(HEADS)], axis=-1)
    x1 = x_ref[...] + _dot(gated, wout_ref[...])
    out_ref[...] = _mlp_residual(x1, _rms(x1, gain_ref[...]).astype(BF16), wup_ref, wdn_ref)


def _hgrn_out_mlp_decode(o, og, onorm, x, w_out, gain, w_up, w_dn):
    args = (o, og, onorm, x, w_out, gain, w_up, w_dn)
    return pl.pallas_call(
        _hgrn_out_mlp_decode_kernel,
        grid=(1,),
        in_specs=[_const_spec(a.shape) for a in args],
        out_specs=pl.BlockSpec(x.shape, lambda i: (0, 0)),
        out_shape=jax.ShapeDtypeStruct(x.shape, F32),
        compiler_params=pltpu.CompilerParams(dimension_semantics=("arbitrary",), vmem_limit_bytes=VMEM_LIMIT),
        name="hgrn_out_mlp_decode",
    )(*args)


def _pool_project(z, wgrp_ref, scale_ref):
    parts = [_dot(z[:, g * POOL_GW:(g + 1) * POOL_GW].astype(BF16), wgrp_ref[g]) for g in range(len(POOL_WINDOWS))]
    return jnp.concatenate(parts, axis=-1) * scale_ref[...]


def _shift_rows(a, k):
    return pltpu.roll(a, k, axis=0)


def _pool_prompt_kernel(x_ref, halo_ref, pgain_ref, wgrp_ref, scale_ref, gain_ref, wup_ref, wdn_ref, fgain_ref,
                        y_ref, hist_ref, x1_ref, hn_ref, pre_ref, *, tm, tiles_per_seq, n_tiles):
    s = pl.program_id(0)
    t = jnp.minimum(s, n_tiles - 1) % tiles_per_seq
    cur, prev = s % 2, (s + 1) % 2
    assert D_FF // FF_CHUNK == len(POOL_WINDOWS) == 4

    def finish():
        y_ref[0] = _rms(pre_ref[...], fgain_ref[...])

    def mixer_and_mlp(with_mlp):
        x = x_ref[0]
        u = _rms(x, pgain_ref[...])
        halo = jnp.where(t > 0, _rms(halo_ref[0], pgain_ref[...]), 0.0)
        ext = jnp.concatenate([halo, u], axis=0)
        pos = t * tm + lax.broadcasted_iota(jnp.int32, (tm, 1), 0)

        def pool_group(g):
            w = POOL_WINDOWS[g]
            cols = slice(g * POOL_GW, (g + 1) * POOL_GW)
            win = ext[:, cols]
            k = 1
            while k < w:
                win = win + _shift_rows(win, k)
                k *= 2
            cnt = jnp.minimum(w, pos + 1).astype(F32)
            z = (win[HALO:, :] / cnt - u[:, cols]).astype(BF16)
            return x[:, cols] + _dot(z, wgrp_ref[g]) * scale_ref[:, cols]

        def mlp_chunk(c, hn, acc):
            ff = slice(c * FF_CHUNK, (c + 1) * FF_CHUNK)
            a = jnp.square(jnp.maximum(_dot(hn, wup_ref[:, ff]), 0.0)).astype(BF16)
            return acc + _dot(a, wdn_ref[ff, :])

        if with_mlp:
            hn_prev = hn_ref[prev]
            acc = mlp_chunk(0, hn_prev, x1_ref[prev])
        parts = [pool_group(0), pool_group(1)]
        if with_mlp:
            acc = mlp_chunk(1, hn_prev, acc)
        parts += [pool_group(2), pool_group(3)]
        if with_mlp:
            acc = mlp_chunk(2, hn_prev, acc)
        x1_new = jnp.concatenate(parts, axis=-1)
        x1_ref[cur] = x1_new
        hn_ref[cur] = _rms(x1_new, gain_ref[...]).astype(BF16)
        if with_mlp:
            pre_ref[...] = mlp_chunk(3, hn_prev, acc)

        @pl.when(t == tiles_per_seq - 1)
        def _():
            hist_ref[0] = ext[HALO + tm - POOL_HIST:, :]

    def head():
        pre_ref[...] = jnp.zeros_like(pre_ref)
        mixer_and_mlp(False)

    def full():
        finish()
        mixer_and_mlp(True)

    lax.cond(s == 0, head, lambda: lax.cond(s == n_tiles + 1, finish, full))


def _pool_prompt(x, pgain, wgrp, scale, gain, w_up, w_dn, fgain, tm):
    b, l, _ = x.shape
    vec = _const_spec((1, D_MODEL))
    tps = l // tm
    n_tiles = b * tps
    blocks_per_tile = tm // HALO
    cur = lambda s: jnp.minimum(s, n_tiles - 1)
    done = lambda s: jnp.maximum(s - 2, 0)
    return pl.pallas_call(
        functools.partial(_pool_prompt_kernel, tm=tm, tiles_per_seq=tps, n_tiles=n_tiles),
        grid=(n_tiles + 2,),
        in_specs=[pl.BlockSpec((1, tm, D_MODEL), lambda s: (cur(s) // tps, cur(s) % tps, 0)),
                  pl.BlockSpec((1, HALO, D_MODEL),
                               lambda s: (cur(s) // tps, jnp.maximum((cur(s) % tps) * blocks_per_tile - 1, 0), 0)),
                  vec, _const_spec(wgrp.shape), vec, vec, _const_spec(w_up.shape), _const_spec(w_dn.shape), vec],
        out_specs=[pl.BlockSpec((1, tm, D_MODEL), lambda s: (done(s) // tps, done(s) % tps, 0)),
                   pl.BlockSpec((1, POOL_HIST, D_MODEL), lambda s: (cur(s) // tps, 0, 0))],
        out_shape=[jax.ShapeDtypeStruct((b, l, D_MODEL), F32), jax.ShapeDtypeStruct((b, POOL_HIST, D_MODEL), F32)],
        scratch_shapes=[pltpu.VMEM((2, tm, D_MODEL), F32), pltpu.VMEM((2, tm, D_MODEL), BF16),
                        pltpu.VMEM((tm, D_MODEL), F32)],
        compiler_params=pltpu.CompilerParams(dimension_semantics=("arbitrary",), vmem_limit_bytes=VMEM_LIMIT),
        name="pool_mlp_prompt",
    )(x, x, pgain, wgrp, scale, gain, w_up, w_dn, fgain)


def _pool_decode_kernel(x_ref, hist_ref, pgain_ref, wgrp_ref, scale_ref, gain_ref, wup_ref, wdn_ref, fgain_ref,
                        y_ref, hist_out_ref):
    x = x_ref[...]
    u = _rms(x, pgain_ref[...])
    parts = []
    for g, w in enumerate(POOL_WINDOWS):
        cols = slice(g * POOL_GW, (g + 1) * POOL_GW)
        acc = u[:, cols]
        for j in range(1, w):
            acc = acc + hist_ref[POOL_HIST - j, :, cols]
        parts.append(acc / float(min(w, PAST_LEN + 1)) - u[:, cols])
    z = jnp.concatenate(parts, axis=-1)
    x1 = x + _pool_project(z, wgrp_ref, scale_ref)
    hn = _rms(x1, gain_ref[...]).astype(BF16)
    y_ref[...] = _rms(_mlp_residual(x1, hn, wup_ref, wdn_ref), fgain_ref[...])
    hist_out_ref[0:POOL_HIST - 1] = hist_ref[1:POOL_HIST]
    hist_out_ref[POOL_HIST - 1] = u


def _pool_decode(x, hist, pgain, wgrp, scale, gain, w_up, w_dn, fgain):
    n = x.shape[0]
    vec = _const_spec((1, D_MODEL))
    return pl.pallas_call(
        _pool_decode_kernel,
        grid=(1,),
        in_specs=[_const_spec(x.shape), _const_spec(hist.shape), vec, _const_spec(wgrp.shape), vec, vec,
                  _const_spec(w_up.shape), _const_spec(w_dn.shape), vec],
        out_specs=[pl.BlockSpec(x.shape, lambda i: (0, 0)), pl.BlockSpec(hist.shape, lambda i: (0, 0, 0))],
        out_shape=[jax.ShapeDtypeStruct(x.shape, F32), jax.ShapeDtypeStruct(hist.shape, F32)],
        compiler_params=pltpu.CompilerParams(dimension_semantics=("arbitrary",), vmem_limit_bytes=VMEM_LIMIT),
        name="pool_mlp_decode",
    )(x, hist, pgain, wgrp, scale, gain, w_up, w_dn, fgain)


def kernel(x_prompt, x_sample, state_hgrn, state_pool, hgrn_norm, hgrn_w_in, hgrn_lb, hgrn_onorm, hgrn_w_out,
           pool_norm, pool_w, pool_scale, mlp_norm, mlp_up, mlp_down, final_norm):
    b, l, _ = x_prompt.shape
    n = x_sample.shape[0]
    vec = lambda a: a.reshape(1, D_MODEL)
    onorm = vec(hgrn_onorm[0])
    xp = x_prompt.reshape(b * l, D_MODEL)
    xs = x_sample.reshape(n, D_MODEL)
    tm = 512
    steps = b * l // tm

    w_in, qt, ft, v, og = _inproj_decode(xs, vec(hgrn_norm[0]), hgrn_lb, hgrn_w_in)

    gated, hgrn_p, w_up0, w_dn0, w_out = _hgrn_mix(xp, vec(hgrn_norm[0]), hgrn_lb, w_in, onorm,
                                                   ((mlp_up, 0), (mlp_down, 0), (hgrn_w_out, 0)), b, tm)
    pool_w_rows = pool_w.reshape(pool_w.shape[0], D_MODEL, POOL_GW)
    x2, w_up1, w_dn1, w_pool, o_dec, hgrn_s = _hgrn_out_mlp(
        gated, xp, w_out, vec(mlp_norm[0]), w_up0, w_dn0, tm, ((mlp_up, 1), (mlp_down, 1), (pool_w_rows, 0)),
        (qt, ft, v.reshape(steps, n // steps, D_MODEL), state_hgrn[0]))
    w_pool = w_pool.reshape(len(POOL_WINDOWS), POOL_GW, POOL_GW)
    pool_args = (vec(pool_norm[0]), w_pool, vec(pool_scale[0]), vec(mlp_norm[1]), w_up1, w_dn1, vec(final_norm))
    y_prompt, pool_p = _pool_prompt(x2.reshape(b, l, D_MODEL), *pool_args, tm=tm)

    x2 = _hgrn_out_mlp_decode(o_dec.reshape(n, D_MODEL), og, onorm, xs, w_out, vec(mlp_norm[0]), w_up0, w_dn0)
    y_sample, pool_s = _pool_decode(x2, jnp.transpose(state_pool[0], (1, 0, 2)), *pool_args)

    return (y_prompt, y_sample.reshape(n, 1, D_MODEL), hgrn_p[None], hgrn_s[None], pool_p[None],
            jnp.transpose(pool_s, (1, 0, 2))[None])
```

```python
import functools

import jax
import jax.numpy as jnp
from jax import lax
from jax.experimental import pallas as pl
from jax.experimental.pallas import tpu as pltpu

D_MODEL = 1024
HEADS = 8
HEAD_DIM = 128
SUBLANES = 8
D_FF = 4 * D_MODEL
POOL_WINDOWS = (2, 4, 8, 16)
POOL_GW = D_MODEL // len(POOL_WINDOWS)
POOL_HIST = max(POOL_WINDOWS) - 1
HALO = 16
EPS = 1e-6
PAST_LEN = 16384

CHUNK = 128
LOG_DECAY_LIMIT = 80.0
FF_CHUNK = 1024
VMEM_LIMIT = 56 * 1024 * 1024

BF16 = jnp.bfloat16
F32 = jnp.float32


def _const_spec(shape):
    nd = len(shape)
    return pl.BlockSpec(shape, lambda *_: (0,) * nd, pipeline_mode=pl.Buffered(1))


def _rms(x, gain):
    return x * lax.rsqrt(jnp.mean(x * x, axis=-1, keepdims=True) + EPS) * gain


def _sigmoid(x):
    return 1.0 / (1.0 + jnp.exp(-x))


def _dot(a, b):
    return jnp.dot(a, b, preferred_element_type=F32)


def _layer_lower_bound(lb_ref):
    lbr = lb_ref[...]
    e = jnp.exp(lbr - jnp.max(lbr, axis=0, keepdims=True))
    return e[0:1, :] / jnp.sum(e, axis=0, keepdims=True)


def _inproj_activation(j, p, lb):
    if j == 0:
        return p * _sigmoid(p) * (HEAD_DIM ** -0.5)
    if j == 1:
        return lb + (1.0 - lb) * _sigmoid(p)
    if j == 2:
        return p
    return _sigmoid(p)


def _inproj_group(j, u, w_ref, lb):
    return _inproj_activation(j, _dot(u, w_ref[:, j * D_MODEL:(j + 1) * D_MODEL]), lb)


def _inproj_decode_kernel(x_ref, gain_ref, lb_ref, w_ref, wb_ref, qt_ref, ft_ref, v_ref, og_ref):
    j = pl.program_id(0)
    wb = w_ref[...].astype(BF16)
    wb_ref[...] = wb
    p = _dot(_rms(x_ref[...], gain_ref[...]).astype(BF16), wb)
    lb = _layer_lower_bound(lb_ref)

    def heads_transposed(val, out_ref):
        for h in range(HEADS):
            out_ref[h] = val[:, h * HEAD_DIM:(h + 1) * HEAD_DIM].T

    @pl.when(j == 0)
    def _():
        heads_transposed(_inproj_activation(0, p, lb), qt_ref)

    @pl.when(j == 1)
    def _():
        heads_transposed(_inproj_activation(1, p, lb), ft_ref)

    @pl.when(j == 2)
    def _():
        v_ref[...] = p

    @pl.when(j == 3)
    def _():
        og_ref[...] = _inproj_activation(3, p, lb)


def _inproj_decode(x, gain, lb, w_in_f32):
    n = x.shape[0]
    vec = _const_spec((1, D_MODEL))
    tok = pl.BlockSpec((n, D_MODEL), lambda j: (0, 0))
    per_head = pl.BlockSpec((HEADS, HEAD_DIM, n), lambda j: (0, 0, 0))
    return pl.pallas_call(
        _inproj_decode_kernel,
        grid=(4,),
        in_specs=[_const_spec(x.shape), vec, _const_spec(lb.shape),
                  pl.BlockSpec((None, D_MODEL, D_MODEL), lambda j: (0, 0, j))],
        out_specs=[pl.BlockSpec((D_MODEL, D_MODEL), lambda j: (0, j)), per_head, per_head, tok, tok],
        out_shape=[jax.ShapeDtypeStruct((D_MODEL, 4 * D_MODEL), BF16),
                   jax.ShapeDtypeStruct((HEADS, HEAD_DIM, n), F32), jax.ShapeDtypeStruct((HEADS, HEAD_DIM, n), F32),
                   jax.ShapeDtypeStruct((n, D_MODEL), F32), jax.ShapeDtypeStruct((n, D_MODEL), F32)],
        compiler_params=pltpu.CompilerParams(dimension_semantics=("arbitrary",), vmem_limit_bytes=VMEM_LIMIT),
        name="hgrn_inproj_decode",
    )(x, gain, lb, w_in_f32)


def _head_gate(o, onorm, og):
    return (o * lax.rsqrt(jnp.mean(o * o, axis=-1, keepdims=True) + EPS) * onorm * og).astype(BF16)


def _lower_triangle():
    return (lax.broadcasted_iota(jnp.int32, (CHUNK, CHUNK), 0)
            >= lax.broadcasted_iota(jnp.int32, (CHUNK, CHUNK), 1))


def _cumsum_rows(g):
    n = g.shape[0]
    row = lax.broadcasted_iota(jnp.int32, g.shape, 0)
    s = 1
    while s < n:
        g = g + jnp.where(row >= s, pltpu.roll(g, s, axis=0), 0.0)
        s *= 2
    return g


def _rec_chunk_factored(q, k, big_g, v, st):
    tri = _lower_triangle()
    decay = jnp.exp(big_g)
    qd = (q * decay).astype(BF16)
    kd = (k * jnp.exp(-big_g)).astype(BF16)
    a = lax.dot_general(qd, kd, (((1,), (1,)), ((), ())), preferred_element_type=F32)
    a = jnp.where(tri, a, 0.0).astype(BF16)
    o = _dot(a, v) + lax.dot_general(qd, st.astype(BF16), (((1,), (1,)), ((), ())), preferred_element_type=F32)
    ds = lax.dot_general(v, kd, (((0,), (0,)), ((), ())), preferred_element_type=F32)
    return o, (st + ds) * decay[CHUNK - 1:CHUNK, :]


def _rec_chunk_stepwise(q_ref, k_ref, lg_ref, v, st):
    vt = v.astype(F32).T
    lane = lax.broadcasted_iota(jnp.int32, (1, CHUNK), 1)

    def token_group(grp, carry):
        st, ot = carry
        rows = pl.ds(pl.multiple_of(grp * SUBLANES, SUBLANES), SUBLANES)
        lg, k, q = lg_ref[rows, :], k_ref[rows, :], q_ref[rows, :]
        for i in range(SUBLANES):
            onehot = (lane == grp * SUBLANES + i).astype(F32)
            vcol = jnp.sum(vt * onehot, axis=1, keepdims=True)
            st = st * jnp.exp(lg[i:i + 1, :]) + vcol * k[i:i + 1, :]
            ocol = jnp.sum(st * q[i:i + 1, :], axis=1, keepdims=True)
            ot = ot + ocol * onehot
        return st, ot

    st, ot = lax.fori_loop(0, CHUNK // SUBLANES, token_group, (st, jnp.zeros((HEAD_DIM, CHUNK), F32)))
    return ot.T, st


def _cast_slabs(src_refs, dst_refs):
    for src, dst in zip(src_refs, dst_refs):
        dst[...] = src[...].astype(BF16)


def _slab_specs(weights, n_steps):
    ins, outs, shapes = [], [], []
    for w, layer in weights:
        _, rows, cols = w.shape
        slab = rows // n_steps
        assert slab * n_steps == rows and slab % 16 == 0
        ins.append(pl.BlockSpec((None, slab, cols), lambda s, layer=layer: (layer, jnp.minimum(s, n_steps - 1), 0)))
        outs.append(pl.BlockSpec((slab, cols), lambda s: (jnp.minimum(s, n_steps - 1), 0)))
        shapes.append(jax.ShapeDtypeStruct((rows, cols), BF16))
    return ins, outs, shapes


def _hgrn_mix_kernel(x_ref, gain_ref, lb_ref, w_ref, onorm_ref, wa_ref, wb_ref, wc_ref,
                     gated_ref, s_out_ref, wa_out, wb_out, wc_out,
                     q_s, k_s, lg_s, v_s, og_s, st_ref, slow_ref, *, tm, tiles_per_seq, n_tiles):
    s = pl.program_id(0)
    _cast_slabs((wa_ref, wb_ref, wc_ref), (wa_out, wb_out, wc_out))
    cur, prev = s % 2, (s + 1) % 2
    t_prev = jnp.maximum(s - 1, 0) % tiles_per_seq
    n_chunks = tm // CHUNK
    assert n_chunks == 4

    @pl.when(s == 0)
    def _():
        for ref in (q_s, k_s, lg_s, v_s, og_s):
            ref[1] = jnp.zeros(ref.shape[1:], ref.dtype)
        slow_ref[0] = 0

    @pl.when(t_prev == 0)
    def _():
        st_ref[...] = jnp.zeros_like(st_ref)

    def project(j, r, u, lb):
        rs = slice(r * (tm // 2), (r + 1) * (tm // 2))
        val = _inproj_group(j, u[rs, :], w_ref, lb)
        if j == 0:
            q_s[cur, rs, :] = val
        elif j == 1:
            k_s[cur, rs, :] = 1.0 - val
            lg_s[cur, rs, :] = jnp.log(val)
        elif j == 2:
            v_s[cur, rs, :] = val.astype(BF16)
        else:
            og_s[cur, rs, :] = val

    def recur(rows, heads, big_g):
        for h in heads:
            cols = slice(h * HEAD_DIM, (h + 1) * HEAD_DIM)
            v = v_s[prev, rows, cols]
            if big_g is None:
                o, st = _rec_chunk_stepwise(q_s.at[prev, rows, cols], k_s.at[prev, rows, cols],
                                            lg_s.at[prev, rows, cols], v, st_ref[h])
            else:
                o, st = _rec_chunk_factored(q_s[prev, rows, cols], k_s[prev, rows, cols], big_g[:, cols], v, st_ref[h])
            st_ref[h] = st
            gated_ref[rows, cols] = _head_gate(o, onorm_ref[:, cols], og_s[prev, rows, cols])

    def body(stepwise):
        u = _rms(x_ref[...], gain_ref[...]).astype(BF16)
        lb = _layer_lower_bound(lb_ref)
        for j in range(4):
            rows = slice(j * CHUNK, (j + 1) * CHUNK)
            big_g = None if stepwise else _cumsum_rows(lg_s[prev, rows, :])
            for r in range(2):
                project(j, r, u, lb)
                if not stepwise:
                    recur(rows, range(r * HEADS // 2, (r + 1) * HEADS // 2), big_g)
        if stepwise:

            def chunk(c, carry):
                recur(pl.ds(pl.multiple_of(c * CHUNK, CHUNK), CHUNK), range(HEADS), None)
                return carry

            lax.fori_loop(0, n_chunks, chunk, 0)
        worst = None
        for c in range(n_chunks):
            tot = jnp.sum(lg_s[cur, c * CHUNK:(c + 1) * CHUNK, :], axis=0, keepdims=True)
            worst = tot if worst is None else jnp.minimum(worst, tot)
        slow_ref[0] = (jnp.min(worst) < -LOG_DECAY_LIMIT).astype(jnp.int32)

        @pl.when(t_prev == tiles_per_seq - 1)
        def _():
            for h in range(HEADS):
                s_out_ref[0, h] = st_ref[h].T

    lax.cond(slow_ref[0] == 0, lambda: body(False), lambda: body(True))


def _hgrn_mix(x, gain, lb, w_in, onorm, later_weights, seqs, tm):
    m = x.shape[0]
    n_tiles = m // tm
    w_ins, w_outs, w_shapes = _slab_specs(later_weights, n_tiles)
    tps = n_tiles // seqs
    cur = lambda s: jnp.minimum(s, n_tiles - 1)
    prev = lambda s: jnp.maximum(s - 1, 0)
    vec = _const_spec((1, D_MODEL))
    buf = lambda dt: pltpu.VMEM((2, tm, D_MODEL), dt)
    return pl.pallas_call(
        functools.partial(_hgrn_mix_kernel, tm=tm, tiles_per_seq=tps, n_tiles=n_tiles),
        grid=(n_tiles + 1,),
        in_specs=[pl.BlockSpec((tm, D_MODEL), lambda s: (cur(s), 0)), vec, _const_spec(lb.shape),
                  _const_spec(w_in.shape), vec] + w_ins,
        out_specs=[pl.BlockSpec((tm, D_MODEL), lambda s: (prev(s), 0)),
                   pl.BlockSpec((1, HEADS, HEAD_DIM, HEAD_DIM), lambda s: (prev(s) // tps, 0, 0, 0))] + w_outs,
        out_shape=[jax.ShapeDtypeStruct((m, D_MODEL), BF16),
                   jax.ShapeDtypeStruct((seqs, HEADS, HEAD_DIM, HEAD_DIM), F32)] + w_shapes,
        scratch_shapes=[buf(F32), buf(F32), buf(F32), buf(BF16), buf(F32),
                        pltpu.VMEM((HEADS, HEAD_DIM, HEAD_DIM), F32), pltpu.SMEM((1,), jnp.int32)],
        compiler_params=pltpu.CompilerParams(dimension_semantics=("arbitrary",), vmem_limit_bytes=VMEM_LIMIT),
        name="hgrn_mix_prompt",
    )(x, gain, lb, w_in, onorm, *[w for w, _ in later_weights])


DEC_PER_STEP = 4


def _decode_state_update(step, j, qt_ref, ft_ref, v_ref, s_ref, o_ref, s_out_ref):
    shift = (HEAD_DIM - (DEC_PER_STEP * step + j)) % HEAD_DIM
    for h in range(HEADS):
        cols = slice(h * HEAD_DIM, (h + 1) * HEAD_DIM)
        fcol = pltpu.roll(ft_ref[h], shift, axis=1)[:, 0:1]
        qcol = pltpu.roll(qt_ref[h], shift, axis=1)[:, 0:1]
        vrow = v_ref[j:j + 1, cols]
        s_new = vrow + fcol * (s_ref[j, h] - vrow)
        s_out_ref[j, h] = s_new
        o_ref[j:j + 1, cols] = jnp.sum(qcol * s_new, axis=0, keepdims=True)


def _hgrn_out_mlp_kernel(gated_ref, x_ref, wout_ref, gain_ref, wup_ref, wdn_ref, *rest, n_cast):
    cast_in, side_in = rest[:n_cast], rest[n_cast:n_cast + 4]
    out_ref = rest[n_cast + 4]
    cast_out, side_out = rest[n_cast + 5:2 * n_cast + 5], rest[2 * n_cast + 5:]
    _cast_slabs(cast_in, cast_out)
    x1 = x_ref[...] + _dot(gated_ref[...], wout_ref[...])
    hn = _rms(x1, gain_ref[...]).astype(BF16)
    acc = x1
    assert D_FF // FF_CHUNK == DEC_PER_STEP
    for c in range(D_FF // FF_CHUNK):
        cols = slice(c * FF_CHUNK, (c + 1) * FF_CHUNK)
        a = jnp.square(jnp.maximum(_dot(hn, wup_ref[:, cols]), 0.0)).astype(BF16)
        acc = acc + _dot(a, wdn_ref[cols, :])
        _decode_state_update(pl.program_id(0), c, *side_in, *side_out)
    out_ref[...] = acc


def _hgrn_out_mlp(gated, x, w_out, gain, w_up, w_dn, tm, later_weights, decode):
    m = x.shape[0]
    steps = m // tm
    qt, ft, v, s0 = decode
    assert v.shape == (steps, DEC_PER_STEP, D_MODEL) and s0.shape[0] == steps * DEC_PER_STEP
    row = pl.BlockSpec((tm, D_MODEL), lambda i: (i, 0))
    w_ins, w_outs, w_shapes = _slab_specs(later_weights, steps)
    tok = pl.BlockSpec((None, DEC_PER_STEP, D_MODEL), lambda i: (i, 0, 0))
    state = pl.BlockSpec((DEC_PER_STEP, HEADS, HEAD_DIM, HEAD_DIM), lambda i: (i, 0, 0, 0))
    return pl.pallas_call(
        functools.partial(_hgrn_out_mlp_kernel, n_cast=len(later_weights)),
        grid=(steps,),
        in_specs=[row, row, _const_spec(w_out.shape), _const_spec((1, D_MODEL)), _const_spec(w_up.shape),
                  _const_spec(w_dn.shape)] + w_ins + [_const_spec(qt.shape), _const_spec(ft.shape), tok, state],
        out_specs=[row] + w_outs + [tok, state],
        out_shape=[jax.ShapeDtypeStruct((m, D_MODEL), F32)] + w_shapes
                  + [jax.ShapeDtypeStruct(v.shape, F32), jax.ShapeDtypeStruct(s0.shape, F32)],
        compiler_params=pltpu.CompilerParams(dimension_semantics=("arbitrary",), vmem_limit_bytes=VMEM_LIMIT),
        name="hgrn_out_mlp",
    )(gated, x, w_out, gain, w_up, w_dn, *[w for w, _ in later_weights], qt, ft, v, s0)


def _ff_chunk_specs(w_up, w_dn):
    return [pl.BlockSpec((w_up.shape[0], FF_CHUNK), lambda c: (0, c)),
            pl.BlockSpec((FF_CHUNK, w_dn.shape[1]), lambda c: (c, 0))]


def _mlp_chunk_accumulate(acc_ref, hn_ref, wup_ref, wdn_ref):
    a = jnp.square(jnp.maximum(_dot(hn_ref[...], wup_ref[...]), 0.0)).astype(BF16)
    acc_ref[...] += _dot(a, wdn_ref[...])


def _hgrn_out_mlp_decode_kernel(o_ref, og_ref, onorm_ref, x_ref, wout_ref, gain_ref, wup_ref, wdn_ref, out_ref,
                                hn_ref):
    @pl.when(pl.program_id(0) == 0)
    def _():
        gated = jnp.concatenate(
            [_head_gate(o_ref[:, h * HEAD_DIM:(h + 1) * HEAD_DIM], onorm_ref[:, h * HEAD_DIM:(h + 1) * HEAD_DIM],
                        og_ref[:, h * HEAD_DIM:(h + 1) * HEAD_DIM]) for h in range(HEADS)], axis=-1)
        x1 = x_ref[...] + _dot(gated, wout_ref[...])
        out_ref[...] = x1
        hn_ref[...] = _rms(x1, gain_ref[...]).astype(BF16)

    _mlp_chunk_accumulate(out_ref, hn_ref, wup_ref, wdn_ref)


def _hgrn_out_mlp_decode(o, og, onorm, x, w_out, gain, w_up, w_dn):
    resident = (o, og, onorm, x, w_out, gain)
    return pl.pallas_call(
        _hgrn_out_mlp_decode_kernel,
        grid=(D_FF // FF_CHUNK,),
        in_specs=[_const_spec(a.shape) for a in resident] + _ff_chunk_specs(w_up, w_dn),
        out_specs=pl.BlockSpec(x.shape, lambda c: (0, 0)),
        out_shape=jax.ShapeDtypeStruct(x.shape, F32),
        scratch_shapes=[pltpu.VMEM(x.shape, BF16)],
        compiler_params=pltpu.CompilerParams(dimension_semantics=("arbitrary",), vmem_limit_bytes=VMEM_LIMIT),
        name="hgrn_out_mlp_decode",
    )(*resident, w_up, w_dn)


def _pool_project(z, wgrp_ref, scale_ref):
    parts = [_dot(z[:, g * POOL_GW:(g + 1) * POOL_GW].astype(BF16), wgrp_ref[g]) for g in range(len(POOL_WINDOWS))]
    return jnp.concatenate(parts, axis=-1) * scale_ref[...]


def _shift_rows(a, k):
    return pltpu.roll(a, k, axis=0)


def _pool_prompt_kernel(x_ref, halo_ref, pgain_ref, wgrp_ref, scale_ref, gain_ref, wup_ref, wdn_ref, fgain_ref,
                        y_ref, hist_ref, x1_ref, hn_ref, pre_ref, *, tm, tiles_per_seq, n_tiles):
    s = pl.program_id(0)
    t = jnp.minimum(s, n_tiles - 1) % tiles_per_seq
    cur, prev = s % 2, (s + 1) % 2
    assert D_FF // FF_CHUNK == len(POOL_WINDOWS) == 4

    def finish():
        y_ref[0] = _rms(pre_ref[...], fgain_ref[...])

    def mixer_and_mlp(with_mlp):
        x = x_ref[0]
        u = _rms(x, pgain_ref[...])
        halo = jnp.where(t > 0, _rms(halo_ref[0], pgain_ref[...]), 0.0)
        ext = jnp.concatenate([halo, u], axis=0)
        pos = t * tm + lax.broadcasted_iota(jnp.int32, (tm, 1), 0)

        def pool_group(g):
            w = POOL_WINDOWS[g]
            cols = slice(g * POOL_GW, (g + 1) * POOL_GW)
            win = ext[:, cols]
            k = 1
            while k < w:
                win = win + _shift_rows(win, k)
                k *= 2
            cnt = jnp.minimum(w, pos + 1).astype(F32)
            z = (win[HALO:, :] / cnt - u[:, cols]).astype(BF16)
            return x[:, cols] + _dot(z, wgrp_ref[g]) * scale_ref[:, cols]

        def mlp_chunk(c, hn, acc):
            ff = slice(c * FF_CHUNK, (c + 1) * FF_CHUNK)
            a = jnp.square(jnp.maximum(_dot(hn, wup_ref[:, ff]), 0.0)).astype(BF16)
            return acc + _dot(a, wdn_ref[ff, :])

        if with_mlp:
            hn_prev = hn_ref[prev]
            acc = mlp_chunk(0, hn_prev, x1_ref[prev])
        parts = [pool_group(0), pool_group(1)]
        if with_mlp:
            acc = mlp_chunk(1, hn_prev, acc)
        parts += [pool_group(2), pool_group(3)]
        if with_mlp:
            acc = mlp_chunk(2, hn_prev, acc)
        x1_new = jnp.concatenate(parts, axis=-1)
        x1_ref[cur] = x1_new
        hn_ref[cur] = _rms(x1_new, gain_ref[...]).astype(BF16)
        if with_mlp:
            pre_ref[...] = mlp_chunk(3, hn_prev, acc)

        @pl.when(t == tiles_per_seq - 1)
        def _():
            hist_ref[0] = ext[HALO + tm - POOL_HIST:, :]

    def head():
        pre_ref[...] = jnp.zeros_like(pre_ref)
        mixer_and_mlp(False)

    def full():
        finish()
        mixer_and_mlp(True)

    lax.cond(s == 0, head, lambda: lax.cond(s == n_tiles + 1, finish, full))


def _pool_prompt(x, pgain, wgrp, scale, gain, w_up, w_dn, fgain, tm):
    b, l, _ = x.shape
    vec = _const_spec((1, D_MODEL))
    tps = l // tm
    n_tiles = b * tps
    blocks_per_tile = tm // HALO
    cur = lambda s: jnp.minimum(s, n_tiles - 1)
    done = lambda s: jnp.maximum(s - 2, 0)
    return pl.pallas_call(
        functools.partial(_pool_prompt_kernel, tm=tm, tiles_per_seq=tps, n_tiles=n_tiles),
        grid=(n_tiles + 2,),
        in_specs=[pl.BlockSpec((1, tm, D_MODEL), lambda s: (cur(s) // tps, cur(s) % tps, 0)),
                  pl.BlockSpec((1, HALO, D_MODEL),
                               lambda s: (cur(s) // tps, jnp.maximum((cur(s) % tps) * blocks_per_tile - 1, 0), 0)),
                  vec, _const_spec(wgrp.shape), vec, vec, _const_spec(w_up.shape), _const_spec(w_dn.shape), vec],
        out_specs=[pl.BlockSpec((1, tm, D_MODEL), lambda s: (done(s) // tps, done(s) % tps, 0)),
                   pl.BlockSpec((1, POOL_HIST, D_MODEL), lambda s: (cur(s) // tps, 0, 0))],
        out_shape=[jax.ShapeDtypeStruct((b, l, D_MODEL), F32), jax.ShapeDtypeStruct((b, POOL_HIST, D_MODEL), F32)],
        scratch_shapes=[pltpu.VMEM((2, tm, D_MODEL), F32), pltpu.VMEM((2, tm, D_MODEL), BF16),
                        pltpu.VMEM((tm, D_MODEL), F32)],
        compiler_params=pltpu.CompilerParams(dimension_semantics=("arbitrary",), vmem_limit_bytes=VMEM_LIMIT),
        name="pool_mlp_prompt",
    )(x, x, pgain, wgrp, scale, gain, w_up, w_dn, fgain)


def _pool_decode_kernel(x_ref, hist_ref, pgain_ref, wgrp_ref, scale_ref, gain_ref, fgain_ref, wup_ref, wdn_ref,
                        y_ref, hist_out_ref, hn_ref):
    c = pl.program_id(0)

    @pl.when(c == 0)
    def _():
        x = x_ref[...]
        u = _rms(x, pgain_ref[...])
        parts = []
        for g, w in enumerate(POOL_WINDOWS):
            cols = slice(g * POOL_GW, (g + 1) * POOL_GW)
            acc = u[:, cols]
            for j in range(1, w):
                acc = acc + hist_ref[POOL_HIST - j, :, cols]
            parts.append(acc / float(min(w, PAST_LEN + 1)) - u[:, cols])
        z = jnp.concatenate(parts, axis=-1)
        x1 = x + _pool_project(z, wgrp_ref, scale_ref)
        y_ref[...] = x1
        hn_ref[...] = _rms(x1, gain_ref[...]).astype(BF16)
        hist_out_ref[0:POOL_HIST - 1] = hist_ref[1:POOL_HIST]
        hist_out_ref[POOL_HIST - 1] = u

    _mlp_chunk_accumulate(y_ref, hn_ref, wup_ref, wdn_ref)

    @pl.when(c == pl.num_programs(0) - 1)
    def _():
        y_ref[...] = _rms(y_ref[...], fgain_ref[...])


def _pool_decode(x, hist, pgain, wgrp, scale, gain, w_up, w_dn, fgain):
    vec = _const_spec((1, D_MODEL))
    return pl.pallas_call(
        _pool_decode_kernel,
        grid=(D_FF // FF_CHUNK,),
        in_specs=[_const_spec(x.shape), _const_spec(hist.shape), vec, _const_spec(wgrp.shape), vec, vec, vec]
                 + _ff_chunk_specs(w_up, w_dn),
        out_specs=[pl.BlockSpec(x.shape, lambda c: (0, 0)), pl.BlockSpec(hist.shape, lambda c: (0, 0, 0))],
        out_shape=[jax.ShapeDtypeStruct(x.shape, F32), jax.ShapeDtypeStruct(hist.shape, F32)],
        scratch_shapes=[pltpu.VMEM(x.shape, BF16)],
        compiler_params=pltpu.CompilerParams(dimension_semantics=("arbitrary",), vmem_limit_bytes=VMEM_LIMIT),
        name="pool_mlp_decode",
    )(x, hist, pgain, wgrp, scale, gain, fgain, w_up, w_dn)


def kernel(x_prompt, x_sample, state_hgrn, state_pool, hgrn_norm, hgrn_w_in, hgrn_lb, hgrn_onorm, hgrn_w_out,
           pool_norm, pool_w, pool_scale, mlp_norm, mlp_up, mlp_down, final_norm):
    b, l, _ = x_prompt.shape
    n = x_sample.shape[0]
    vec = lambda a: a.reshape(1, D_MODEL)
    onorm = vec(hgrn_onorm[0])
    xp = x_prompt.reshape(b * l, D_MODEL)
    xs = x_sample.reshape(n, D_MODEL)
    tm = 512
    steps = b * l // tm

    w_in, qt, ft, v, og = _inproj_decode(xs, vec(hgrn_norm[0]), hgrn_lb, hgrn_w_in)

    gated, hgrn_p, w_up0, w_dn0, w_out = _hgrn_mix(xp, vec(hgrn_norm[0]), hgrn_lb, w_in, onorm,
                                                   ((mlp_up, 0), (mlp_down, 0), (hgrn_w_out, 0)), b, tm)
    pool_w_rows = pool_w.reshape(pool_w.shape[0], D_MODEL, POOL_GW)
    x2, w_up1, w_dn1, w_pool, o_dec, hgrn_s = _hgrn_out_mlp(
        gated, xp, w_out, vec(mlp_norm[0]), w_up0, w_dn0, tm, ((mlp_up, 1), (mlp_down, 1), (pool_w_rows, 0)),
        (qt, ft, v.reshape(steps, n // steps, D_MODEL), state_hgrn[0]))
    w_pool = w_pool.reshape(len(POOL_WINDOWS), POOL_GW, POOL_GW)
    pool_args = (vec(pool_norm[0]), w_pool, vec(pool_scale[0]), vec(mlp_norm[1]), w_up1, w_dn1, vec(final_norm))
    y_prompt, pool_p = _pool_prompt(x2.reshape(b, l, D_MODEL), *pool_args, tm=tm)

    x2 = _hgrn_out_mlp_decode(o_dec.reshape(n, D_MODEL), og, onorm, xs, w_out, vec(mlp_norm[0]), w_up0, w_dn0)
    y_sample, pool_s = _pool_decode(x2, jnp.transpose(state_pool[0], (1, 0, 2)), *pool_args)

    return (y_prompt, y_sample.reshape(n, 1, D_MODEL), hgrn_p[None], hgrn_s[None], pool_p[None],
            jnp.transpose(pool_s, (1, 0, 2))[None])
```

```python
import functools

import jax
import jax.numpy as jnp
from jax import lax
from jax.experimental import pallas as pl
from jax.experimental.pallas import tpu as pltpu

D_MODEL = 1024
HEADS = 8
HEAD_DIM = 128
SUBLANES = 8
D_FF = 4 * D_MODEL
POOL_WINDOWS = (2, 4, 8, 16)
POOL_GW = D_MODEL // len(POOL_WINDOWS)
POOL_HIST = max(POOL_WINDOWS) - 1
HALO = 16
EPS = 1e-6
PAST_LEN = 16384

PROMPT_TILE = 512
CHUNK = 128
LOG_DECAY_LIMIT = 80.0
FF_CHUNK = 1024
VMEM_LIMIT = 56 * 1024 * 1024

BF16 = jnp.bfloat16
F32 = jnp.float32


def _const_spec(shape):
    nd = len(shape)
    return pl.BlockSpec(shape, lambda *_: (0,) * nd, pipeline_mode=pl.Buffered(1))


def _rms(x, gain):
    return x * lax.rsqrt(jnp.mean(x * x, axis=-1, keepdims=True) + EPS) * gain


def _sigmoid(x):
    return 1.0 / (1.0 + jnp.exp(-x))


def _dot(a, b):
    return jnp.dot(a, b, preferred_element_type=F32)


def _layer_lower_bound(lb_ref):
    lbr = lb_ref[...]
    e = jnp.exp(lbr - jnp.max(lbr, axis=0, keepdims=True))
    return e[0:1, :] / jnp.sum(e, axis=0, keepdims=True)


def _inproj_activation(j, p, lb):
    if j == 0:
        return p * _sigmoid(p) * (HEAD_DIM ** -0.5)
    if j == 1:
        return lb + (1.0 - lb) * _sigmoid(p)
    if j == 2:
        return p
    return _sigmoid(p)


def _inproj_group(j, u, w_ref, lb):
    return _inproj_activation(j, _dot(u, w_ref[:, j * D_MODEL:(j + 1) * D_MODEL]), lb)


def _inproj_decode_kernel(x_ref, gain_ref, lb_ref, w_ref, wb_ref, qt_ref, ft_ref, v_ref, og_ref):
    j = pl.program_id(0)
    wb = w_ref[...].astype(BF16)
    wb_ref[...] = wb
    p = _dot(_rms(x_ref[...], gain_ref[...]).astype(BF16), wb)
    lb = _layer_lower_bound(lb_ref)

    def heads_transposed(val, out_ref):
        for h in range(HEADS):
            out_ref[h] = val[:, h * HEAD_DIM:(h + 1) * HEAD_DIM].T

    @pl.when(j == 0)
    def _():
        heads_transposed(_inproj_activation(0, p, lb), qt_ref)

    @pl.when(j == 1)
    def _():
        heads_transposed(_inproj_activation(1, p, lb), ft_ref)

    @pl.when(j == 2)
    def _():
        v_ref[...] = p

    @pl.when(j == 3)
    def _():
        og_ref[...] = _inproj_activation(3, p, lb)


def _inproj_decode(x, gain, lb, w_in_f32):
    n = x.shape[0]
    vec = _const_spec((1, D_MODEL))
    tok = pl.BlockSpec((n, D_MODEL), lambda j: (0, 0))
    per_head = pl.BlockSpec((HEADS, HEAD_DIM, n), lambda j: (0, 0, 0))
    return pl.pallas_call(
        _inproj_decode_kernel,
        grid=(4,),
        in_specs=[_const_spec(x.shape), vec, _const_spec(lb.shape),
                  pl.BlockSpec((None, D_MODEL, D_MODEL), lambda j: (0, 0, j))],
        out_specs=[pl.BlockSpec((D_MODEL, D_MODEL), lambda j: (0, j)), per_head, per_head, tok, tok],
        out_shape=[jax.ShapeDtypeStruct((D_MODEL, 4 * D_MODEL), BF16),
                   jax.ShapeDtypeStruct((HEADS, HEAD_DIM, n), F32), jax.ShapeDtypeStruct((HEADS, HEAD_DIM, n), F32),
                   jax.ShapeDtypeStruct((n, D_MODEL), F32), jax.ShapeDtypeStruct((n, D_MODEL), F32)],
        compiler_params=pltpu.CompilerParams(dimension_semantics=("arbitrary",), vmem_limit_bytes=VMEM_LIMIT),
        name="hgrn_inproj_decode",
    )(x, gain, lb, w_in_f32)


def _head_gate(o, onorm, og):
    return (o * lax.rsqrt(jnp.mean(o * o, axis=-1, keepdims=True) + EPS) * onorm * og).astype(BF16)


def _lower_triangle():
    return (lax.broadcasted_iota(jnp.int32, (CHUNK, CHUNK), 0)
            >= lax.broadcasted_iota(jnp.int32, (CHUNK, CHUNK), 1))


def _cumsum_rows(g):
    n = g.shape[0]
    row = lax.broadcasted_iota(jnp.int32, g.shape, 0)
    s = 1
    while s < n:
        g = g + jnp.where(row >= s, pltpu.roll(g, s, axis=0), 0.0)
        s *= 2
    return g


def _rec_chunk_factored(q, k, big_g, v, st):
    tri = _lower_triangle()
    decay = jnp.exp(big_g)
    qd = (q * decay).astype(BF16)
    kd = (k * jnp.exp(-big_g)).astype(BF16)
    a = lax.dot_general(qd, kd, (((1,), (1,)), ((), ())), preferred_element_type=F32)
    a = jnp.where(tri, a, 0.0).astype(BF16)
    o = _dot(a, v) + lax.dot_general(qd, st.astype(BF16), (((1,), (1,)), ((), ())), preferred_element_type=F32)
    ds = lax.dot_general(v, kd, (((0,), (0,)), ((), ())), preferred_element_type=F32)
    return o, (st + ds) * decay[CHUNK - 1:CHUNK, :]


def _rec_chunk_stepwise(q_ref, k_ref, lg_ref, v, st):
    vt = v.astype(F32).T
    lane = lax.broadcasted_iota(jnp.int32, (1, CHUNK), 1)

    def token_group(grp, carry):
        st, ot = carry
        rows = pl.ds(pl.multiple_of(grp * SUBLANES, SUBLANES), SUBLANES)
        lg, k, q = lg_ref[rows, :], k_ref[rows, :], q_ref[rows, :]
        for i in range(SUBLANES):
            onehot = (lane == grp * SUBLANES + i).astype(F32)
            vcol = jnp.sum(vt * onehot, axis=1, keepdims=True)
            st = st * jnp.exp(lg[i:i + 1, :]) + vcol * k[i:i + 1, :]
            ocol = jnp.sum(st * q[i:i + 1, :], axis=1, keepdims=True)
            ot = ot + ocol * onehot
        return st, ot

    st, ot = lax.fori_loop(0, CHUNK // SUBLANES, token_group, (st, jnp.zeros((HEAD_DIM, CHUNK), F32)))
    return ot.T, st


def _cast_slabs(src_refs, dst_refs):
    for src, dst in zip(src_refs, dst_refs):
        dst[...] = src[...].astype(BF16)


def _slab_specs(weights, n_steps):
    ins, outs, shapes = [], [], []
    for w, layer in weights:
        _, rows, cols = w.shape
        slab = rows // n_steps
        assert slab * n_steps == rows and slab % 16 == 0
        ins.append(pl.BlockSpec((None, slab, cols), lambda s, layer=layer: (layer, jnp.minimum(s, n_steps - 1), 0)))
        outs.append(pl.BlockSpec((slab, cols), lambda s: (jnp.minimum(s, n_steps - 1), 0)))
        shapes.append(jax.ShapeDtypeStruct((rows, cols), BF16))
    return ins, outs, shapes


def _hgrn_mix_kernel(x_ref, gain_ref, lb_ref, w_ref, onorm_ref, wa_ref, wb_ref, wc_ref,
                     gated_ref, s_out_ref, wa_out, wb_out, wc_out,
                     q_s, k_s, lg_s, v_s, og_s, st_ref, slow_ref, *, tm, tiles_per_seq, n_tiles):
    s = pl.program_id(0)
    _cast_slabs((wa_ref, wb_ref, wc_ref), (wa_out, wb_out, wc_out))
    cur, prev = s % 2, (s + 1) % 2
    t_prev = jnp.maximum(s - 1, 0) % tiles_per_seq
    n_chunks = tm // CHUNK
    assert n_chunks == 4

    @pl.when(s == 0)
    def _():
        for ref in (q_s, k_s, lg_s, v_s, og_s):
            ref[1] = jnp.zeros(ref.shape[1:], ref.dtype)
        slow_ref[0] = 0

    @pl.when(t_prev == 0)
    def _():
        st_ref[...] = jnp.zeros_like(st_ref)

    def project(j, r, u, lb):
        rs = slice(r * (tm // 2), (r + 1) * (tm // 2))
        val = _inproj_group(j, u[rs, :], w_ref, lb)
        if j == 0:
            q_s[cur, rs, :] = val
        elif j == 1:
            k_s[cur, rs, :] = 1.0 - val
            lg_s[cur, rs, :] = jnp.log(val)
        elif j == 2:
            v_s[cur, rs, :] = val.astype(BF16)
        else:
            og_s[cur, rs, :] = val

    def recur(rows, heads, big_g):
        for h in heads:
            cols = slice(h * HEAD_DIM, (h + 1) * HEAD_DIM)
            v = v_s[prev, rows, cols]
            if big_g is None:
                o, st = _rec_chunk_stepwise(q_s.at[prev, rows, cols], k_s.at[prev, rows, cols],
                                            lg_s.at[prev, rows, cols], v, st_ref[h])
            else:
                o, st = _rec_chunk_factored(q_s[prev, rows, cols], k_s[prev, rows, cols], big_g[:, cols], v, st_ref[h])
            st_ref[h] = st
            gated_ref[rows, cols] = _head_gate(o, onorm_ref[:, cols], og_s[prev, rows, cols])

    def body(stepwise):
        u = _rms(x_ref[...], gain_ref[...]).astype(BF16)
        lb = _layer_lower_bound(lb_ref)
        for j in range(4):
            rows = slice(j * CHUNK, (j + 1) * CHUNK)
            big_g = None if stepwise else _cumsum_rows(lg_s[prev, rows, :])
            for r in range(2):
                project(j, r, u, lb)
                if not stepwise:
                    recur(rows, range(r * HEADS // 2, (r + 1) * HEADS // 2), big_g)
        if stepwise:

            def chunk(c, carry):
                recur(pl.ds(pl.multiple_of(c * CHUNK, CHUNK), CHUNK), range(HEADS), None)
                return carry

            lax.fori_loop(0, n_chunks, chunk, 0)
        worst = None
        for c in range(n_chunks):
            tot = jnp.sum(lg_s[cur, c * CHUNK:(c + 1) * CHUNK, :], axis=0, keepdims=True)
            worst = tot if worst is None else jnp.minimum(worst, tot)
        slow_ref[0] = (jnp.min(worst) < -LOG_DECAY_LIMIT).astype(jnp.int32)

        @pl.when(t_prev == tiles_per_seq - 1)
        def _():
            for h in range(HEADS):
                s_out_ref[0, h] = st_ref[h].T

    lax.cond(slow_ref[0] == 0, lambda: body(False), lambda: body(True))


def _hgrn_mix(x, gain, lb, w_in, onorm, later_weights, seqs, tm):
    m = x.shape[0]
    n_tiles = m // tm
    w_ins, w_outs, w_shapes = _slab_specs(later_weights, n_tiles)
    tps = n_tiles // seqs
    cur = lambda s: jnp.minimum(s, n_tiles - 1)
    prev = lambda s: jnp.maximum(s - 1, 0)
    vec = _const_spec((1, D_MODEL))
    buf = lambda dt: pltpu.VMEM((2, tm, D_MODEL), dt)
    return pl.pallas_call(
        functools.partial(_hgrn_mix_kernel, tm=tm, tiles_per_seq=tps, n_tiles=n_tiles),
        grid=(n_tiles + 1,),
        in_specs=[pl.BlockSpec((tm, D_MODEL), lambda s: (cur(s), 0)), vec, _const_spec(lb.shape),
                  _const_spec(w_in.shape), vec] + w_ins,
        out_specs=[pl.BlockSpec((tm, D_MODEL), lambda s: (prev(s), 0)),
                   pl.BlockSpec((1, HEADS, HEAD_DIM, HEAD_DIM), lambda s: (prev(s) // tps, 0, 0, 0))] + w_outs,
        out_shape=[jax.ShapeDtypeStruct((m, D_MODEL), BF16),
                   jax.ShapeDtypeStruct((seqs, HEADS, HEAD_DIM, HEAD_DIM), F32)] + w_shapes,
        scratch_shapes=[buf(F32), buf(F32), buf(F32), buf(BF16), buf(F32),
                        pltpu.VMEM((HEADS, HEAD_DIM, HEAD_DIM), F32), pltpu.SMEM((1,), jnp.int32)],
        compiler_params=pltpu.CompilerParams(dimension_semantics=("arbitrary",), vmem_limit_bytes=VMEM_LIMIT),
        name="hgrn_mix_prompt",
    )(x, gain, lb, w_in, onorm, *[w for w, _ in later_weights])


DEC_PER_STEP = 4


def _decode_state_update(step, j, qt_ref, ft_ref, v_ref, s_ref, o_ref, s_out_ref):
    shift = (HEAD_DIM - (DEC_PER_STEP * step + j)) % HEAD_DIM
    for h in range(HEADS):
        cols = slice(h * HEAD_DIM, (h + 1) * HEAD_DIM)
        fcol = pltpu.roll(ft_ref[h], shift, axis=1)[:, 0:1]
        qcol = pltpu.roll(qt_ref[h], shift, axis=1)[:, 0:1]
        vrow = v_ref[j:j + 1, cols]
        s_new = vrow + fcol * (s_ref[j, h] - vrow)
        s_out_ref[j, h] = s_new
        o_ref[j:j + 1, cols] = jnp.sum(qcol * s_new, axis=0, keepdims=True)


def _hgrn_out_mlp_kernel(gated_ref, x_ref, wout_ref, gain_ref, wup_ref, wdn_ref, *rest, n_cast):
    cast_in, side_in = rest[:n_cast], rest[n_cast:n_cast + 4]
    out_ref = rest[n_cast + 4]
    cast_out, side_out = rest[n_cast + 5:2 * n_cast + 5], rest[2 * n_cast + 5:]
    _cast_slabs(cast_in, cast_out)
    x1 = x_ref[...] + _dot(gated_ref[...], wout_ref[...])
    hn = _rms(x1, gain_ref[...]).astype(BF16)
    acc = x1
    assert D_FF // FF_CHUNK == DEC_PER_STEP
    for c in range(D_FF // FF_CHUNK):
        cols = slice(c * FF_CHUNK, (c + 1) * FF_CHUNK)
        a = jnp.square(jnp.maximum(_dot(hn, wup_ref[:, cols]), 0.0)).astype(BF16)
        acc = acc + _dot(a, wdn_ref[cols, :])
        _decode_state_update(pl.program_id(0), c, *side_in, *side_out)
    out_ref[...] = acc


def _hgrn_out_mlp(gated, x, w_out, gain, w_up, w_dn, tm, later_weights, decode):
    m = x.shape[0]
    steps = m // tm
    qt, ft, v, s0 = decode
    assert v.shape == (steps, DEC_PER_STEP, D_MODEL) and s0.shape[0] == steps * DEC_PER_STEP
    row = pl.BlockSpec((tm, D_MODEL), lambda i: (i, 0))
    w_ins, w_outs, w_shapes = _slab_specs(later_weights, steps)
    tok = pl.BlockSpec((None, DEC_PER_STEP, D_MODEL), lambda i: (i, 0, 0))
    state = pl.BlockSpec((DEC_PER_STEP, HEADS, HEAD_DIM, HEAD_DIM), lambda i: (i, 0, 0, 0))
    return pl.pallas_call(
        functools.partial(_hgrn_out_mlp_kernel, n_cast=len(later_weights)),
        grid=(steps,),
        in_specs=[row, row, _const_spec(w_out.shape), _const_spec((1, D_MODEL)), _const_spec(w_up.shape),
                  _const_spec(w_dn.shape)] + w_ins + [_const_spec(qt.shape), _const_spec(ft.shape), tok, state],
        out_specs=[row] + w_outs + [tok, state],
        out_shape=[jax.ShapeDtypeStruct((m, D_MODEL), F32)] + w_shapes
                  + [jax.ShapeDtypeStruct(v.shape, F32), jax.ShapeDtypeStruct(s0.shape, F32)],
        compiler_params=pltpu.CompilerParams(dimension_semantics=("arbitrary",), vmem_limit_bytes=VMEM_LIMIT),
        name="hgrn_out_mlp",
    )(gated, x, w_out, gain, w_up, w_dn, *[w for w, _ in later_weights], qt, ft, v, s0)


def _ff_chunk_specs(w_up, w_dn):
    return [pl.BlockSpec((w_up.shape[0], FF_CHUNK), lambda c: (0, c)),
            pl.BlockSpec((FF_CHUNK, w_dn.shape[1]), lambda c: (c, 0))]


def _mlp_chunk_accumulate(acc_ref, hn_ref, wup_ref, wdn_ref):
    a = jnp.square(jnp.maximum(_dot(hn_ref[...], wup_ref[...]), 0.0)).astype(BF16)
    acc_ref[...] += _dot(a, wdn_ref[...])


def _hgrn_out_mlp_decode_kernel(o_ref, og_ref, onorm_ref, x_ref, wout_ref, gain_ref, wup_ref, wdn_ref, out_ref,
                                hn_ref):
    @pl.when(pl.program_id(0) == 0)
    def _():
        gated = jnp.concatenate(
            [_head_gate(o_ref[:, h * HEAD_DIM:(h + 1) * HEAD_DIM], onorm_ref[:, h * HEAD_DIM:(h + 1) * HEAD_DIM],
                        og_ref[:, h * HEAD_DIM:(h + 1) * HEAD_DIM]) for h in range(HEADS)], axis=-1)
        x1 = x_ref[...] + _dot(gated, wout_ref[...])
        out_ref[...] = x1
        hn_ref[...] = _rms(x1, gain_ref[...]).astype(BF16)

    _mlp_chunk_accumulate(out_ref, hn_ref, wup_ref, wdn_ref)


def _hgrn_out_mlp_decode(o, og, onorm, x, w_out, gain, w_up, w_dn):
    resident = (o, og, onorm, x, w_out, gain)
    return pl.pallas_call(
        _hgrn_out_mlp_decode_kernel,
        grid=(D_FF // FF_CHUNK,),
        in_specs=[_const_spec(a.shape) for a in resident] + _ff_chunk_specs(w_up, w_dn),
        out_specs=pl.BlockSpec(x.shape, lambda c: (0, 0)),
        out_shape=jax.ShapeDtypeStruct(x.shape, F32),
        scratch_shapes=[pltpu.VMEM(x.shape, BF16)],
        compiler_params=pltpu.CompilerParams(dimension_semantics=("arbitrary",), vmem_limit_bytes=VMEM_LIMIT),
        name="hgrn_out_mlp_decode",
    )(*resident, w_up, w_dn)


def _pool_project(z, wgrp_ref, scale_ref):
    parts = [_dot(z[:, g * POOL_GW:(g + 1) * POOL_GW].astype(BF16), wgrp_ref[g]) for g in range(len(POOL_WINDOWS))]
    return jnp.concatenate(parts, axis=-1) * scale_ref[...]


def _shift_rows(a, k):
    return pltpu.roll(a, k, axis=0)


def _pool_prompt_kernel(x_ref, halo_ref, pgain_ref, wgrp_ref, scale_ref, gain_ref, wup_ref, wdn_ref, fgain_ref,
                        y_ref, hist_ref, x1_ref, hn_ref, pre_ref, *, tm, tiles_per_seq, n_tiles):
    s = pl.program_id(0)
    t = jnp.minimum(s, n_tiles - 1) % tiles_per_seq
    cur, prev = s % 2, (s + 1) % 2
    assert D_FF // FF_CHUNK == len(POOL_WINDOWS) == 4

    def finish():
        y_ref[0] = _rms(pre_ref[...], fgain_ref[...])

    def mixer_and_mlp(with_mlp):
        x = x_ref[0]
        u = _rms(x, pgain_ref[...])
        halo = jnp.where(t > 0, _rms(halo_ref[0], pgain_ref[...]), 0.0)
        ext = jnp.concatenate([halo, u], axis=0)
        pos = t * tm + lax.broadcasted_iota(jnp.int32, (tm, 1), 0)

        def pool_group(g):
            w = POOL_WINDOWS[g]
            cols = slice(g * POOL_GW, (g + 1) * POOL_GW)
            win = ext[:, cols]
            k = 1
            while k < w:
                win = win + _shift_rows(win, k)
                k *= 2
            cnt = jnp.minimum(w, pos + 1).astype(F32)
            z = (win[HALO:, :] / cnt - u[:, cols]).astype(BF16)
            return x[:, cols] + _dot(z, wgrp_ref[g]) * scale_ref[:, cols]

        def mlp_chunk(c, hn, acc):
            ff = slice(c * FF_CHUNK, (c + 1) * FF_CHUNK)
            a = jnp.square(jnp.maximum(_dot(hn, wup_ref[:, ff]), 0.0)).astype(BF16)
            return acc + _dot(a, wdn_ref[ff, :])

        if with_mlp:
            hn_prev = hn_ref[prev]
            acc = mlp_chunk(0, hn_prev, x1_ref[prev])
        parts = [pool_group(0), pool_group(1)]
        if with_mlp:
            acc = mlp_chunk(1, hn_prev, acc)
        parts += [pool_group(2), pool_group(3)]
        if with_mlp:
            acc = mlp_chunk(2, hn_prev, acc)
        x1_new = jnp.concatenate(parts, axis=-1)
        x1_ref[cur] = x1_new
        hn_ref[cur] = _rms(x1_new, gain_ref[...]).astype(BF16)
        if with_mlp:
            pre_ref[...] = mlp_chunk(3, hn_prev, acc)

        @pl.when(t == tiles_per_seq - 1)
        def _():
            hist_ref[0] = ext[HALO + tm - POOL_HIST:, :]

    def head():
        pre_ref[...] = jnp.zeros_like(pre_ref)
        mixer_and_mlp(False)

    def full():
        finish()
        mixer_and_mlp(True)

    lax.cond(s == 0, head, lambda: lax.cond(s == n_tiles + 1, finish, full))


def _pool_prompt(x, pgain, wgrp, scale, gain, w_up, w_dn, fgain, tm):
    b, l, _ = x.shape
    vec = _const_spec((1, D_MODEL))
    tps = l // tm
    n_tiles = b * tps
    blocks_per_tile = tm // HALO
    cur = lambda s: jnp.minimum(s, n_tiles - 1)
    done = lambda s: jnp.maximum(s - 2, 0)
    return pl.pallas_call(
        functools.partial(_pool_prompt_kernel, tm=tm, tiles_per_seq=tps, n_tiles=n_tiles),
        grid=(n_tiles + 2,),
        in_specs=[pl.BlockSpec((1, tm, D_MODEL), lambda s: (cur(s) // tps, cur(s) % tps, 0)),
                  pl.BlockSpec((1, HALO, D_MODEL),
                               lambda s: (cur(s) // tps, jnp.maximum((cur(s) % tps) * blocks_per_tile - 1, 0), 0)),
                  vec, _const_spec(wgrp.shape), vec, vec, _const_spec(w_up.shape), _const_spec(w_dn.shape), vec],
        out_specs=[pl.BlockSpec((1, tm, D_MODEL), lambda s: (done(s) // tps, done(s) % tps, 0)),
                   pl.BlockSpec((1, POOL_HIST, D_MODEL), lambda s: (cur(s) // tps, 0, 0))],
        out_shape=[jax.ShapeDtypeStruct((b, l, D_MODEL), F32), jax.ShapeDtypeStruct((b, POOL_HIST, D_MODEL), F32)],
        scratch_shapes=[pltpu.VMEM((2, tm, D_MODEL), F32), pltpu.VMEM((2, tm, D_MODEL), BF16),
                        pltpu.VMEM((tm, D_MODEL), F32)],
        compiler_params=pltpu.CompilerParams(dimension_semantics=("arbitrary",), vmem_limit_bytes=VMEM_LIMIT),
        name="pool_mlp_prompt",
    )(x, x, pgain, wgrp, scale, gain, w_up, w_dn, fgain)


def _pool_decode_kernel(x_ref, hist_ref, pgain_ref, wgrp_ref, scale_ref, gain_ref, fgain_ref, wup_ref, wdn_ref,
                        y_ref, hist_out_ref, hn_ref):
    c = pl.program_id(0)

    @pl.when(c == 0)
    def _():
        x = x_ref[...]
        u = _rms(x, pgain_ref[...])
        parts = []
        for g, w in enumerate(POOL_WINDOWS):
            cols = slice(g * POOL_GW, (g + 1) * POOL_GW)
            acc = u[:, cols]
            for j in range(1, w):
                acc = acc + hist_ref[POOL_HIST - j, :, cols]
            parts.append(acc / float(min(w, PAST_LEN + 1)) - u[:, cols])
        z = jnp.concatenate(parts, axis=-1)
        x1 = x + _pool_project(z, wgrp_ref, scale_ref)
        y_ref[...] = x1
        hn_ref[...] = _rms(x1, gain_ref[...]).astype(BF16)
        hist_out_ref[0:POOL_HIST - 1] = hist_ref[1:POOL_HIST]
        hist_out_ref[POOL_HIST - 1] = u

    _mlp_chunk_accumulate(y_ref, hn_ref, wup_ref, wdn_ref)

    @pl.when(c == pl.num_programs(0) - 1)
    def _():
        y_ref[...] = _rms(y_ref[...], fgain_ref[...])


def _pool_decode(x, hist, pgain, wgrp, scale, gain, w_up, w_dn, fgain):
    vec = _const_spec((1, D_MODEL))
    return pl.pallas_call(
        _pool_decode_kernel,
        grid=(D_FF // FF_CHUNK,),
        in_specs=[_const_spec(x.shape), _const_spec(hist.shape), vec, _const_spec(wgrp.shape), vec, vec, vec]
                 + _ff_chunk_specs(w_up, w_dn),
        out_specs=[pl.BlockSpec(x.shape, lambda c: (0, 0)), pl.BlockSpec(hist.shape, lambda c: (0, 0, 0))],
        out_shape=[jax.ShapeDtypeStruct(x.shape, F32), jax.ShapeDtypeStruct(hist.shape, F32)],
        scratch_shapes=[pltpu.VMEM(x.shape, BF16)],
        compiler_params=pltpu.CompilerParams(dimension_semantics=("arbitrary",), vmem_limit_bytes=VMEM_LIMIT),
        name="pool_mlp_decode",
    )(x, hist, pgain, wgrp, scale, gain, fgain, w_up, w_dn)


def kernel(x_prompt, x_sample, state_hgrn, state_pool, hgrn_norm, hgrn_w_in, hgrn_lb, hgrn_onorm, hgrn_w_out,
           pool_norm, pool_w, pool_scale, mlp_norm, mlp_up, mlp_down, final_norm):
    b, l, _ = x_prompt.shape
    n = x_sample.shape[0]
    vec = lambda a: a.reshape(1, D_MODEL)
    onorm = vec(hgrn_onorm[0])
    xp = x_prompt.reshape(b * l, D_MODEL)
    xs = x_sample.reshape(n, D_MODEL)
    tm = PROMPT_TILE
    steps = b * l // tm

    w_in, qt, ft, v, og = _inproj_decode(xs, vec(hgrn_norm[0]), hgrn_lb, hgrn_w_in)

    gated, hgrn_p, w_up0, w_dn0, w_out = _hgrn_mix(xp, vec(hgrn_norm[0]), hgrn_lb, w_in, onorm,
                                                   ((mlp_up, 0), (mlp_down, 0), (hgrn_w_out, 0)), b, tm)
    pool_w_rows = pool_w.reshape(pool_w.shape[0], D_MODEL, POOL_GW)
    x2, w_up1, w_dn1, w_pool, o_dec, hgrn_s = _hgrn_out_mlp(
        gated, xp, w_out, vec(mlp_norm[0]), w_up0, w_dn0, tm, ((mlp_up, 1), (mlp_down, 1), (pool_w_rows, 0)),
        (qt, ft, v.reshape(steps, n // steps, D_MODEL), state_hgrn[0]))
    w_pool = w_pool.reshape(len(POOL_WINDOWS), POOL_GW, POOL_GW)
    pool_args = (vec(pool_norm[0]), w_pool, vec(pool_scale[0]), vec(mlp_norm[1]), w_up1, w_dn1, vec(final_norm))
    y_prompt, pool_p = _pool_prompt(x2.reshape(b, l, D_MODEL), *pool_args, tm=tm)

    x2 = _hgrn_out_mlp_decode(o_dec.reshape(n, D_MODEL), og, onorm, xs, w_out, vec(mlp_norm[0]), w_up0, w_dn0)
    y_sample, pool_s = _pool_decode(x2, jnp.transpose(state_pool[0], (1, 0, 2)), *pool_args)

    return (y_prompt, y_sample.reshape(n, 1, D_MODEL), hgrn_p[None], hgrn_s[None], pool_p[None],
            jnp.transpose(pool_s, (1, 0, 2))[None])
```

```python
import functools

import jax
import jax.numpy as jnp
from jax import lax
from jax.experimental import pallas as pl
from jax.experimental.pallas import tpu as pltpu

D_MODEL = 1024
HEADS = 8
HEAD_DIM = 128
SUBLANES = 8
D_FF = 4 * D_MODEL
POOL_WINDOWS = (2, 4, 8, 16)
POOL_GW = D_MODEL // len(POOL_WINDOWS)
POOL_HIST = max(POOL_WINDOWS) - 1
HALO = 16
EPS = 1e-6
PAST_LEN = 16384

PROMPT_TILE = 512
CHUNK = 128
LOG_DECAY_LIMIT = 80.0
FF_CHUNK = 1024
VMEM_LIMIT = 56 * 1024 * 1024

BF16 = jnp.bfloat16
F32 = jnp.float32


def _const_spec(shape):
    nd = len(shape)
    return pl.BlockSpec(shape, lambda *_: (0,) * nd, pipeline_mode=pl.Buffered(1))


def _rms(x, gain):
    return x * lax.rsqrt(jnp.mean(x * x, axis=-1, keepdims=True) + EPS) * gain


def _sigmoid(x):
    return 1.0 / (1.0 + jnp.exp(-x))


def _dot(a, b):
    return jnp.dot(a, b, preferred_element_type=F32)


def _layer_lower_bound(lb_ref):
    lbr = lb_ref[...]
    e = jnp.exp(lbr - jnp.max(lbr, axis=0, keepdims=True))
    return e[0:1, :] / jnp.sum(e, axis=0, keepdims=True)


def _inproj_activation(j, p, lb):
    if j == 0:
        return p * _sigmoid(p) * (HEAD_DIM ** -0.5)
    if j == 1:
        return lb + (1.0 - lb) * _sigmoid(p)
    if j == 2:
        return p
    return _sigmoid(p)


def _inproj_group(j, u, w_ref, lb):
    return _inproj_activation(j, _dot(u, w_ref[:, j * D_MODEL:(j + 1) * D_MODEL]), lb)


def _inproj_decode_kernel(x_ref, gain_ref, lb_ref, w_ref, wb_ref, qt_ref, ft_ref, v_ref, og_ref):
    j = pl.program_id(0)
    wb = w_ref[...].astype(BF16)
    wb_ref[...] = wb
    p = _dot(_rms(x_ref[...], gain_ref[...]).astype(BF16), wb)
    lb = _layer_lower_bound(lb_ref)

    def heads_transposed(val, out_ref):
        for h in range(HEADS):
            out_ref[h] = val[:, h * HEAD_DIM:(h + 1) * HEAD_DIM].T

    @pl.when(j == 0)
    def _():
        heads_transposed(_inproj_activation(0, p, lb), qt_ref)

    @pl.when(j == 1)
    def _():
        heads_transposed(_inproj_activation(1, p, lb), ft_ref)

    @pl.when(j == 2)
    def _():
        v_ref[...] = p

    @pl.when(j == 3)
    def _():
        og_ref[...] = _inproj_activation(3, p, lb)


def _inproj_decode(x, gain, lb, w_in_f32):
    n = x.shape[0]
    vec = _const_spec((1, D_MODEL))
    tok = pl.BlockSpec((n, D_MODEL), lambda j: (0, 0))
    per_head = pl.BlockSpec((HEADS, HEAD_DIM, n), lambda j: (0, 0, 0))
    return pl.pallas_call(
        _inproj_decode_kernel,
        grid=(4,),
        in_specs=[_const_spec(x.shape), vec, _const_spec(lb.shape),
                  pl.BlockSpec((None, D_MODEL, D_MODEL), lambda j: (0, 0, j))],
        out_specs=[pl.BlockSpec((D_MODEL, D_MODEL), lambda j: (0, j)), per_head, per_head, tok, tok],
        out_shape=[jax.ShapeDtypeStruct((D_MODEL, 4 * D_MODEL), BF16),
                   jax.ShapeDtypeStruct((HEADS, HEAD_DIM, n), F32), jax.ShapeDtypeStruct((HEADS, HEAD_DIM, n), F32),
                   jax.ShapeDtypeStruct((n, D_MODEL), F32), jax.ShapeDtypeStruct((n, D_MODEL), F32)],
        compiler_params=pltpu.CompilerParams(dimension_semantics=("arbitrary",), vmem_limit_bytes=VMEM_LIMIT),
        name="hgrn_inproj_decode",
    )(x, gain, lb, w_in_f32)


def _head_gate(o, onorm, og):
    return (o * lax.rsqrt(jnp.mean(o * o, axis=-1, keepdims=True) + EPS) * onorm * og).astype(BF16)


def _lower_triangle():
    return (lax.broadcasted_iota(jnp.int32, (CHUNK, CHUNK), 0)
            >= lax.broadcasted_iota(jnp.int32, (CHUNK, CHUNK), 1))


def _cumsum_rows(g):
    n = g.shape[0]
    row = lax.broadcasted_iota(jnp.int32, g.shape, 0)
    s = 1
    while s < n:
        g = g + jnp.where(row >= s, pltpu.roll(g, s, axis=0), 0.0)
        s *= 2
    return g


def _rec_chunk_factored(q, k, big_g, v, st):
    tri = _lower_triangle()
    decay = jnp.exp(big_g)
    qd = (q * decay).astype(BF16)
    kd = (k * jnp.exp(-big_g)).astype(BF16)
    a = lax.dot_general(qd, kd, (((1,), (1,)), ((), ())), preferred_element_type=F32)
    a = jnp.where(tri, a, 0.0).astype(BF16)
    o = _dot(a, v) + lax.dot_general(qd, st.astype(BF16), (((1,), (1,)), ((), ())), preferred_element_type=F32)
    ds = lax.dot_general(v, kd, (((0,), (0,)), ((), ())), preferred_element_type=F32)
    return o, (st + ds) * decay[CHUNK - 1:CHUNK, :]


def _rec_chunk_stepwise(q_ref, k_ref, lg_ref, v, st):
    vt = v.astype(F32).T
    lane = lax.broadcasted_iota(jnp.int32, (1, CHUNK), 1)

    def token_group(grp, carry):
        st, ot = carry
        rows = pl.ds(pl.multiple_of(grp * SUBLANES, SUBLANES), SUBLANES)
        lg, k, q = lg_ref[rows, :], k_ref[rows, :], q_ref[rows, :]
        for i in range(SUBLANES):
            onehot = (lane == grp * SUBLANES + i).astype(F32)
            vcol = jnp.sum(vt * onehot, axis=1, keepdims=True)
            st = st * jnp.exp(lg[i:i + 1, :]) + vcol * k[i:i + 1, :]
            ocol = jnp.sum(st * q[i:i + 1, :], axis=1, keepdims=True)
            ot = ot + ocol * onehot
        return st, ot

    st, ot = lax.fori_loop(0, CHUNK // SUBLANES, token_group, (st, jnp.zeros((HEAD_DIM, CHUNK), F32)))
    return ot.T, st


def _cast_slabs(src_refs, dst_refs):
    for src, dst in zip(src_refs, dst_refs):
        dst[...] = src[...].astype(BF16)


def _slab_specs(weights, n_steps):
    ins, outs, shapes = [], [], []
    for w, layer in weights:
        _, rows, cols = w.shape
        slab = rows // n_steps
        assert slab * n_steps == rows and slab % 16 == 0
        ins.append(pl.BlockSpec((None, slab, cols), lambda s, layer=layer: (layer, jnp.minimum(s, n_steps - 1), 0)))
        outs.append(pl.BlockSpec((slab, cols), lambda s: (jnp.minimum(s, n_steps - 1), 0)))
        shapes.append(jax.ShapeDtypeStruct((rows, cols), BF16))
    return ins, outs, shapes


def _hgrn_mix_kernel(x_ref, gain_ref, lb_ref, w_ref, onorm_ref, wa_ref, wb_ref, wc_ref,
                     gated_ref, s_out_ref, wa_out, wb_out, wc_out,
                     q_s, k_s, lg_s, v_s, og_s, st_ref, slow_ref, *, tm, tiles_per_seq, n_tiles):
    s = pl.program_id(0)
    _cast_slabs((wa_ref, wb_ref, wc_ref), (wa_out, wb_out, wc_out))
    cur, prev = s % 2, (s + 1) % 2
    t_prev = jnp.maximum(s - 1, 0) % tiles_per_seq
    n_chunks = tm // CHUNK
    assert n_chunks == 4

    @pl.when(s == 0)
    def _():
        for ref in (q_s, k_s, lg_s, v_s, og_s):
            ref[1] = jnp.zeros(ref.shape[1:], ref.dtype)
        slow_ref[0] = 0

    @pl.when(t_prev == 0)
    def _():
        st_ref[...] = jnp.zeros_like(st_ref)

    def project(j, r, u, lb):
        rs = slice(r * (tm // 2), (r + 1) * (tm // 2))
        val = _inproj_group(j, u[rs, :], w_ref, lb)
        if j == 0:
            q_s[cur, rs, :] = val
        elif j == 1:
            k_s[cur, rs, :] = 1.0 - val
            lg_s[cur, rs, :] = jnp.log(val)
        elif j == 2:
            v_s[cur, rs, :] = val.astype(BF16)
        else:
            og_s[cur, rs, :] = val

    def recur(rows, heads, big_g):
        for h in heads:
            cols = slice(h * HEAD_DIM, (h + 1) * HEAD_DIM)
            v = v_s[prev, rows, cols]
            if big_g is None:
                o, st = _rec_chunk_stepwise(q_s.at[prev, rows, cols], k_s.at[prev, rows, cols],
                                            lg_s.at[prev, rows, cols], v, st_ref[h])
            else:
                o, st = _rec_chunk_factored(q_s[prev, rows, cols], k_s[prev, rows, cols], big_g[:, cols], v, st_ref[h])
            st_ref[h] = st
            gated_ref[rows, cols] = _head_gate(o, onorm_ref[:, cols], og_s[prev, rows, cols])

    def body(stepwise):
        u = _rms(x_ref[...], gain_ref[...]).astype(BF16)
        lb = _layer_lower_bound(lb_ref)
        for j in range(4):
            rows = slice(j * CHUNK, (j + 1) * CHUNK)
            big_g = None if stepwise else _cumsum_rows(lg_s[prev, rows, :])
            for r in range(2):
                project(j, r, u, lb)
                if not stepwise:
                    recur(rows, range(r * HEADS // 2, (r + 1) * HEADS // 2), big_g)
        if stepwise:

            def chunk(c, carry):
                recur(pl.ds(pl.multiple_of(c * CHUNK, CHUNK), CHUNK), range(HEADS), None)
                return carry

            lax.fori_loop(0, n_chunks, chunk, 0)
        worst = None
        for c in range(n_chunks):
            tot = jnp.sum(lg_s[cur, c * CHUNK:(c + 1) * CHUNK, :], axis=0, keepdims=True)
            worst = tot if worst is None else jnp.minimum(worst, tot)
        slow_ref[0] = (jnp.min(worst) < -LOG_DECAY_LIMIT).astype(jnp.int32)

        @pl.when(t_prev == tiles_per_seq - 1)
        def _():
            for h in range(HEADS):
                s_out_ref[0, h] = st_ref[h].T

    lax.cond(slow_ref[0] == 0, lambda: body(False), lambda: body(True))


def _hgrn_mix(x, gain, lb, w_in, onorm, later_weights, seqs, tm):
    m = x.shape[0]
    n_tiles = m // tm
    w_ins, w_outs, w_shapes = _slab_specs(later_weights, n_tiles)
    tps = n_tiles // seqs
    cur = lambda s: jnp.minimum(s, n_tiles - 1)
    prev = lambda s: jnp.maximum(s - 1, 0)
    vec = _const_spec((1, D_MODEL))
    buf = lambda dt: pltpu.VMEM((2, tm, D_MODEL), dt)
    return pl.pallas_call(
        functools.partial(_hgrn_mix_kernel, tm=tm, tiles_per_seq=tps, n_tiles=n_tiles),
        grid=(n_tiles + 1,),
        in_specs=[pl.BlockSpec((tm, D_MODEL), lambda s: (cur(s), 0)), vec, _const_spec(lb.shape),
                  _const_spec(w_in.shape), vec] + w_ins,
        out_specs=[pl.BlockSpec((tm, D_MODEL), lambda s: (prev(s), 0)),
                   pl.BlockSpec((1, HEADS, HEAD_DIM, HEAD_DIM), lambda s: (prev(s) // tps, 0, 0, 0))] + w_outs,
        out_shape=[jax.ShapeDtypeStruct((m, D_MODEL), BF16),
                   jax.ShapeDtypeStruct((seqs, HEADS, HEAD_DIM, HEAD_DIM), F32)] + w_shapes,
        scratch_shapes=[buf(F32), buf(F32), buf(F32), buf(BF16), buf(F32),
                        pltpu.VMEM((HEADS, HEAD_DIM, HEAD_DIM), F32), pltpu.SMEM((1,), jnp.int32)],
        compiler_params=pltpu.CompilerParams(dimension_semantics=("arbitrary",), vmem_limit_bytes=VMEM_LIMIT),
        name="hgrn_mix_prompt",
    )(x, gain, lb, w_in, onorm, *[w for w, _ in later_weights])


DEC_PER_STEP = 4


def _decode_state_update(step, j, qt_ref, ft_ref, v_ref, s_ref, s_out_ref, o_all_ref):
    shift = (HEAD_DIM - (DEC_PER_STEP * step + j)) % HEAD_DIM
    pieces = []
    for h in range(HEADS):
        cols = slice(h * HEAD_DIM, (h + 1) * HEAD_DIM)
        fcol = pltpu.roll(ft_ref[h], shift, axis=1)[:, 0:1]
        qcol = pltpu.roll(qt_ref[h], shift, axis=1)[:, 0:1]
        vrow = v_ref[j:j + 1, cols]
        s_new = vrow + fcol * (s_ref[j, h] - vrow)
        s_out_ref[j, h] = s_new
        pieces.append(jnp.sum(qcol * s_new, axis=0, keepdims=True))
    assert SUBLANES == 2 * DEC_PER_STEP
    group = pl.ds(pl.multiple_of((step // 2) * SUBLANES, SUBLANES), SUBLANES)
    mine = lax.broadcasted_iota(jnp.int32, (SUBLANES, 1), 0) == (step % 2) * DEC_PER_STEP + j
    o_all_ref[group, :] = jnp.where(mine, jnp.concatenate(pieces, axis=-1), o_all_ref[group, :])


def _hgrn_out_mlp_kernel(gated_ref, x_ref, wout_ref, gain_ref, wup_ref, wdn_ref, *rest, n_cast):
    cast_in, side_in, (og_ref, onorm_ref, xdec_ref) = rest[:n_cast], rest[n_cast:n_cast + 4], rest[n_cast + 4:n_cast + 7]
    out_ref = rest[n_cast + 7]
    cast_out = rest[n_cast + 8:2 * n_cast + 8]
    s_out_ref, dec_out_ref, o_all_ref = rest[2 * n_cast + 8:]
    i = pl.program_id(0)
    assert D_FF // FF_CHUNK == DEC_PER_STEP

    def mlp(x1, side_work):
        hn = _rms(x1, gain_ref[...]).astype(BF16)
        acc = x1
        for c in range(D_FF // FF_CHUNK):
            cols = slice(c * FF_CHUNK, (c + 1) * FF_CHUNK)
            a = jnp.square(jnp.maximum(_dot(hn, wup_ref[:, cols]), 0.0)).astype(BF16)
            acc = acc + _dot(a, wdn_ref[cols, :])
            side_work(c)
        return acc

    @pl.when(i == 0)
    def _():
        o_all_ref[...] = jnp.zeros_like(o_all_ref)

    def prompt_tile():
        _cast_slabs(cast_in, cast_out)
        x1 = x_ref[...] + _dot(gated_ref[...], wout_ref[...])
        out_ref[...] = mlp(x1, lambda c: _decode_state_update(i, c, *side_in, s_out_ref, o_all_ref))

    def decode_tokens():
        gated = jnp.concatenate(
            [_head_gate(o_all_ref[:, h * HEAD_DIM:(h + 1) * HEAD_DIM], onorm_ref[:, h * HEAD_DIM:(h + 1) * HEAD_DIM],
                        og_ref[:, h * HEAD_DIM:(h + 1) * HEAD_DIM]) for h in range(HEADS)], axis=-1)
        dec_out_ref[...] = mlp(xdec_ref[...] + _dot(gated, wout_ref[...]), lambda c: None)

    lax.cond(i < pl.num_programs(0) - 1, prompt_tile, decode_tokens)


def _hgrn_out_mlp(gated, x, w_out, gain, w_up, w_dn, tm, later_weights, decode):
    m = x.shape[0]
    steps = m // tm
    qt, ft, v, s0, og, onorm, xdec = decode
    n = xdec.shape[0]
    assert v.shape == (steps, DEC_PER_STEP, D_MODEL) and s0.shape[0] == n == steps * DEC_PER_STEP
    last = lambda i: jnp.minimum(i, steps - 1)
    row = pl.BlockSpec((tm, D_MODEL), lambda i: (last(i), 0))
    w_ins, w_outs, w_shapes = _slab_specs(later_weights, steps)
    tok = pl.BlockSpec((None, DEC_PER_STEP, D_MODEL), lambda i: (last(i), 0, 0))
    state = pl.BlockSpec((DEC_PER_STEP, HEADS, HEAD_DIM, HEAD_DIM), lambda i: (last(i), 0, 0, 0))
    dec = pl.BlockSpec((n, D_MODEL), lambda i: (0, 0))
    return pl.pallas_call(
        functools.partial(_hgrn_out_mlp_kernel, n_cast=len(later_weights)),
        grid=(steps + 1,),
        in_specs=[row, row, _const_spec(w_out.shape), _const_spec((1, D_MODEL)), _const_spec(w_up.shape),
                  _const_spec(w_dn.shape)] + w_ins
                 + [_const_spec(qt.shape), _const_spec(ft.shape), tok, state,
                    _const_spec(og.shape), _const_spec(onorm.shape), _const_spec(xdec.shape)],
        out_specs=[row] + w_outs + [state, dec],
        out_shape=[jax.ShapeDtypeStruct((m, D_MODEL), F32)] + w_shapes
                  + [jax.ShapeDtypeStruct(s0.shape, F32), jax.ShapeDtypeStruct((n, D_MODEL), F32)],
        scratch_shapes=[pltpu.VMEM((n, D_MODEL), F32)],
        compiler_params=pltpu.CompilerParams(dimension_semantics=("arbitrary",), vmem_limit_bytes=VMEM_LIMIT),
        name="hgrn_out_mlp",
    )(gated, x, w_out, gain, w_up, w_dn, *[w for w, _ in later_weights], qt, ft, v, s0, og, onorm, xdec)


def _ff_chunk_specs(w_up, w_dn):
    return [pl.BlockSpec((w_up.shape[0], FF_CHUNK), lambda c: (0, c)),
            pl.BlockSpec((FF_CHUNK, w_dn.shape[1]), lambda c: (c, 0))]


def _mlp_chunk_accumulate(acc_ref, hn_ref, wup_ref, wdn_ref):
    a = jnp.square(jnp.maximum(_dot(hn_ref[...], wup_ref[...]), 0.0)).astype(BF16)
    acc_ref[...] += _dot(a, wdn_ref[...])


def _pool_project(z, wgrp_ref, scale_ref):
    parts = [_dot(z[:, g * POOL_GW:(g + 1) * POOL_GW].astype(BF16), wgrp_ref[g]) for g in range(len(POOL_WINDOWS))]
    return jnp.concatenate(parts, axis=-1) * scale_ref[...]


def _shift_rows(a, k):
    return pltpu.roll(a, k, axis=0)


def _pool_prompt_kernel(x_ref, halo_ref, pgain_ref, wgrp_ref, scale_ref, gain_ref, wup_ref, wdn_ref, fgain_ref,
                        y_ref, hist_ref, x1_ref, hn_ref, pre_ref, *, tm, tiles_per_seq, n_tiles):
    s = pl.program_id(0)
    t = jnp.minimum(s, n_tiles - 1) % tiles_per_seq
    cur, prev = s % 2, (s + 1) % 2
    assert D_FF // FF_CHUNK == len(POOL_WINDOWS) == 4

    def finish():
        y_ref[0] = _rms(pre_ref[...], fgain_ref[...])

    def mixer_and_mlp(with_mlp):
        x = x_ref[0]
        u = _rms(x, pgain_ref[...])
        halo = jnp.where(t > 0, _rms(halo_ref[0], pgain_ref[...]), 0.0)
        ext = jnp.concatenate([halo, u], axis=0)
        pos = t * tm + lax.broadcasted_iota(jnp.int32, (tm, 1), 0)

        def pool_group(g):
            w = POOL_WINDOWS[g]
            cols = slice(g * POOL_GW, (g + 1) * POOL_GW)
            win = ext[:, cols]
            k = 1
            while k < w:
                win = win + _shift_rows(win, k)
                k *= 2
            cnt = jnp.minimum(w, pos + 1).astype(F32)
            z = (win[HALO:, :] / cnt - u[:, cols]).astype(BF16)
            return x[:, cols] + _dot(z, wgrp_ref[g]) * scale_ref[:, cols]

        def mlp_chunk(c, hn, acc):
            ff = slice(c * FF_CHUNK, (c + 1) * FF_CHUNK)
            a = jnp.square(jnp.maximum(_dot(hn, wup_ref[:, ff]), 0.0)).astype(BF16)
            return acc + _dot(a, wdn_ref[ff, :])

        if with_mlp:
            hn_prev = hn_ref[prev]
            acc = mlp_chunk(0, hn_prev, x1_ref[prev])
        parts = [pool_group(0), pool_group(1)]
        if with_mlp:
            acc = mlp_chunk(1, hn_prev, acc)
        parts += [pool_group(2), pool_group(3)]
        if with_mlp:
            acc = mlp_chunk(2, hn_prev, acc)
        x1_new = jnp.concatenate(parts, axis=-1)
        x1_ref[cur] = x1_new
        hn_ref[cur] = _rms(x1_new, gain_ref[...]).astype(BF16)
        if with_mlp:
            pre_ref[...] = mlp_chunk(3, hn_prev, acc)

        @pl.when(t == tiles_per_seq - 1)
        def _():
            hist_ref[0] = ext[HALO + tm - POOL_HIST:, :]

    def head():
        pre_ref[...] = jnp.zeros_like(pre_ref)
        mixer_and_mlp(False)

    def full():
        finish()
        mixer_and_mlp(True)

    lax.cond(s == 0, head, lambda: lax.cond(s == n_tiles + 1, finish, full))


def _pool_prompt(x, pgain, wgrp, scale, gain, w_up, w_dn, fgain, tm):
    b, l, _ = x.shape
    vec = _const_spec((1, D_MODEL))
    tps = l // tm
    n_tiles = b * tps
    blocks_per_tile = tm // HALO
    cur = lambda s: jnp.minimum(s, n_tiles - 1)
    done = lambda s: jnp.maximum(s - 2, 0)
    return pl.pallas_call(
        functools.partial(_pool_prompt_kernel, tm=tm, tiles_per_seq=tps, n_tiles=n_tiles),
        grid=(n_tiles + 2,),
        in_specs=[pl.BlockSpec((1, tm, D_MODEL), lambda s: (cur(s) // tps, cur(s) % tps, 0)),
                  pl.BlockSpec((1, HALO, D_MODEL),
                               lambda s: (cur(s) // tps, jnp.maximum((cur(s) % tps) * blocks_per_tile - 1, 0), 0)),
                  vec, _const_spec(wgrp.shape), vec, vec, _const_spec(w_up.shape), _const_spec(w_dn.shape), vec],
        out_specs=[pl.BlockSpec((1, tm, D_MODEL), lambda s: (done(s) // tps, done(s) % tps, 0)),
                   pl.BlockSpec((1, POOL_HIST, D_MODEL), lambda s: (cur(s) // tps, 0, 0))],
        out_shape=[jax.ShapeDtypeStruct((b, l, D_MODEL), F32), jax.ShapeDtypeStruct((b, POOL_HIST, D_MODEL), F32)],
        scratch_shapes=[pltpu.VMEM((2, tm, D_MODEL), F32), pltpu.VMEM((2, tm, D_MODEL), BF16),
                        pltpu.VMEM((tm, D_MODEL), F32)],
        compiler_params=pltpu.CompilerParams(dimension_semantics=("arbitrary",), vmem_limit_bytes=VMEM_LIMIT),
        name="pool_mlp_prompt",
    )(x, x, pgain, wgrp, scale, gain, w_up, w_dn, fgain)


def _pool_decode_kernel(x_ref, hist_ref, pgain_ref, wgrp_ref, scale_ref, gain_ref, fgain_ref, wup_ref, wdn_ref,
                        y_ref, hist_out_ref, hn_ref):
    c = pl.program_id(0)

    @pl.when(c == 0)
    def _():
        x = x_ref[...]
        u = _rms(x, pgain_ref[...])
        parts = []
        for g, w in enumerate(POOL_WINDOWS):
            cols = slice(g * POOL_GW, (g + 1) * POOL_GW)
            acc = u[:, cols]
            for j in range(1, w):
                acc = acc + hist_ref[POOL_HIST - j, :, cols]
            parts.append(acc / float(min(w, PAST_LEN + 1)) - u[:, cols])
        z = jnp.concatenate(parts, axis=-1)
        x1 = x + _pool_project(z, wgrp_ref, scale_ref)
        y_ref[...] = x1
        hn_ref[...] = _rms(x1, gain_ref[...]).astype(BF16)
        hist_out_ref[0:POOL_HIST - 1] = hist_ref[1:POOL_HIST]
        hist_out_ref[POOL_HIST - 1] = u

    _mlp_chunk_accumulate(y_ref, hn_ref, wup_ref, wdn_ref)

    @pl.when(c == pl.num_programs(0) - 1)
    def _():
        y_ref[...] = _rms(y_ref[...], fgain_ref[...])


def _pool_decode(x, hist, pgain, wgrp, scale, gain, w_up, w_dn, fgain):
    vec = _const_spec((1, D_MODEL))
    return pl.pallas_call(
        _pool_decode_kernel,
        grid=(D_FF // FF_CHUNK,),
        in_specs=[_const_spec(x.shape), _const_spec(hist.shape), vec, _const_spec(wgrp.shape), vec, vec, vec]
                 + _ff_chunk_specs(w_up, w_dn),
        out_specs=[pl.BlockSpec(x.shape, lambda c: (0, 0)), pl.BlockSpec(hist.shape, lambda c: (0, 0, 0))],
        out_shape=[jax.ShapeDtypeStruct(x.shape, F32), jax.ShapeDtypeStruct(hist.shape, F32)],
        scratch_shapes=[pltpu.VMEM(x.shape, BF16)],
        compiler_params=pltpu.CompilerParams(dimension_semantics=("arbitrary",), vmem_limit_bytes=VMEM_LIMIT),
        name="pool_mlp_decode",
    )(x, hist, pgain, wgrp, scale, gain, fgain, w_up, w_dn)


def kernel(x_prompt, x_sample, state_hgrn, state_pool, hgrn_norm, hgrn_w_in, hgrn_lb, hgrn_onorm, hgrn_w_out,
           pool_norm, pool_w, pool_scale, mlp_norm, mlp_up, mlp_down, final_norm):
    b, l, _ = x_prompt.shape
    n = x_sample.shape[0]
    vec = lambda a: a.reshape(1, D_MODEL)
    onorm = vec(hgrn_onorm[0])
    xp = x_prompt.reshape(b * l, D_MODEL)
    xs = x_sample.reshape(n, D_MODEL)
    tm = PROMPT_TILE
    steps = b * l // tm

    w_in, qt, ft, v, og = _inproj_decode(xs, vec(hgrn_norm[0]), hgrn_lb, hgrn_w_in)

    gated, hgrn_p, w_up0, w_dn0, w_out = _hgrn_mix(xp, vec(hgrn_norm[0]), hgrn_lb, w_in, onorm,
                                                   ((mlp_up, 0), (mlp_down, 0), (hgrn_w_out, 0)), b, tm)
    pool_w_rows = pool_w.reshape(pool_w.shape[0], D_MODEL, POOL_GW)
    x2, w_up1, w_dn1, w_pool, hgrn_s, x2_dec = _hgrn_out_mlp(
        gated, xp, w_out, vec(mlp_norm[0]), w_up0, w_dn0, tm, ((mlp_up, 1), (mlp_down, 1), (pool_w_rows, 0)),
        (qt, ft, v.reshape(steps, n // steps, D_MODEL), state_hgrn[0], og, onorm, xs))
    w_pool = w_pool.reshape(len(POOL_WINDOWS), POOL_GW, POOL_GW)
    pool_args = (vec(pool_norm[0]), w_pool, vec(pool_scale[0]), vec(mlp_norm[1]), w_up1, w_dn1, vec(final_norm))
    y_prompt, pool_p = _pool_prompt(x2.reshape(b, l, D_MODEL), *pool_args, tm=tm)

    y_sample, pool_s = _pool_decode(x2_dec, jnp.transpose(state_pool[0], (1, 0, 2)), *pool_args)

    return (y_prompt, y_sample.reshape(n, 1, D_MODEL), hgrn_p[None], hgrn_s[None], pool_p[None],
            jnp.transpose(pool_s, (1, 0, 2))[None])
```

```python
import functools

import jax
import jax.numpy as jnp
from jax import lax
from jax.experimental import pallas as pl
from jax.experimental.pallas import tpu as pltpu

D_MODEL = 1024
HEADS = 8
HEAD_DIM = 128
SUBLANES = 8
D_FF = 4 * D_MODEL
POOL_WINDOWS = (2, 4, 8, 16)
POOL_GW = D_MODEL // len(POOL_WINDOWS)
POOL_HIST = max(POOL_WINDOWS) - 1
HALO = 16
EPS = 1e-6
PAST_LEN = 16384

PROMPT_TILE = 512
CHUNK = 128
LOG_DECAY_LIMIT = 80.0
FF_CHUNK = 1024
VMEM_LIMIT = 56 * 1024 * 1024

BF16 = jnp.bfloat16
F32 = jnp.float32


def _const_spec(shape):
    nd = len(shape)
    return pl.BlockSpec(shape, lambda *_: (0,) * nd, pipeline_mode=pl.Buffered(1))


def _rms(x, gain):
    return x * lax.rsqrt(jnp.mean(x * x, axis=-1, keepdims=True) + EPS) * gain


def _sigmoid(x):
    return 1.0 / (1.0 + jnp.exp(-x))


def _dot(a, b):
    return jnp.dot(a, b, preferred_element_type=F32)


def _layer_lower_bound(lb_ref):
    lbr = lb_ref[...]
    e = jnp.exp(lbr - jnp.max(lbr, axis=0, keepdims=True))
    return e[0:1, :] / jnp.sum(e, axis=0, keepdims=True)


def _inproj_activation(j, p, lb):
    if j == 0:
        return p * _sigmoid(p) * (HEAD_DIM ** -0.5)
    if j == 1:
        return lb + (1.0 - lb) * _sigmoid(p)
    if j == 2:
        return p
    return _sigmoid(p)


def _inproj_group(j, u, w_ref, lb):
    return _inproj_activation(j, _dot(u, w_ref[:, j * D_MODEL:(j + 1) * D_MODEL]), lb)


def _inproj_decode_kernel(x_ref, gain_ref, lb_ref, w_ref, wb_ref, qt_ref, ft_ref, v_ref, og_ref):
    j = pl.program_id(0)
    wb = w_ref[...].astype(BF16)
    wb_ref[...] = wb
    p = _dot(_rms(x_ref[...], gain_ref[...]).astype(BF16), wb)
    lb = _layer_lower_bound(lb_ref)

    def heads_transposed(val, out_ref):
        for h in range(HEADS):
            out_ref[h] = val[:, h * HEAD_DIM:(h + 1) * HEAD_DIM].T

    @pl.when(j == 0)
    def _():
        heads_transposed(_inproj_activation(0, p, lb), qt_ref)

    @pl.when(j == 1)
    def _():
        heads_transposed(_inproj_activation(1, p, lb), ft_ref)

    @pl.when(j == 2)
    def _():
        v_ref[...] = p

    @pl.when(j == 3)
    def _():
        og_ref[...] = _inproj_activation(3, p, lb)


def _inproj_decode(x, gain, lb, w_in_f32):
    n = x.shape[0]
    vec = _const_spec((1, D_MODEL))
    tok = pl.BlockSpec((n, D_MODEL), lambda j: (0, 0))
    per_head = pl.BlockSpec((HEADS, HEAD_DIM, n), lambda j: (0, 0, 0))
    return pl.pallas_call(
        _inproj_decode_kernel,
        grid=(4,),
        in_specs=[_const_spec(x.shape), vec, _const_spec(lb.shape),
                  pl.BlockSpec((None, D_MODEL, D_MODEL), lambda j: (0, 0, j))],
        out_specs=[pl.BlockSpec((D_MODEL, D_MODEL), lambda j: (0, j)), per_head, per_head, tok, tok],
        out_shape=[jax.ShapeDtypeStruct((D_MODEL, 4 * D_MODEL), BF16),
                   jax.ShapeDtypeStruct((HEADS, HEAD_DIM, n), F32), jax.ShapeDtypeStruct((HEADS, HEAD_DIM, n), F32),
                   jax.ShapeDtypeStruct((n, D_MODEL), F32), jax.ShapeDtypeStruct((n, D_MODEL), F32)],
        compiler_params=pltpu.CompilerParams(dimension_semantics=("arbitrary",), vmem_limit_bytes=VMEM_LIMIT),
        name="hgrn_inproj_decode",
    )(x, gain, lb, w_in_f32)


def _head_gate(o, onorm, og):
    return (o * lax.rsqrt(jnp.mean(o * o, axis=-1, keepdims=True) + EPS) * onorm * og).astype(BF16)


def _lower_triangle():
    return (lax.broadcasted_iota(jnp.int32, (CHUNK, CHUNK), 0)
            >= lax.broadcasted_iota(jnp.int32, (CHUNK, CHUNK), 1))


def _cumsum_rows(g):
    n = g.shape[0]
    row = lax.broadcasted_iota(jnp.int32, g.shape, 0)
    s = 1
    while s < n:
        g = g + jnp.where(row >= s, pltpu.roll(g, s, axis=0), 0.0)
        s *= 2
    return g


def _rec_chunk_factored(q, k, big_g, v, st):
    tri = _lower_triangle()
    decay = jnp.exp(big_g)
    qd = (q * decay).astype(BF16)
    kd = (k * jnp.exp(-big_g)).astype(BF16)
    a = lax.dot_general(qd, kd, (((1,), (1,)), ((), ())), preferred_element_type=F32)
    a = jnp.where(tri, a, 0.0).astype(BF16)
    o = _dot(a, v) + lax.dot_general(qd, st.astype(BF16), (((1,), (1,)), ((), ())), preferred_element_type=F32)
    ds = lax.dot_general(v, kd, (((0,), (0,)), ((), ())), preferred_element_type=F32)
    return o, (st + ds) * decay[CHUNK - 1:CHUNK, :]


def _rec_chunk_stepwise(q_ref, k_ref, lg_ref, v, st):
    vt = v.astype(F32).T
    lane = lax.broadcasted_iota(jnp.int32, (1, CHUNK), 1)

    def token_group(grp, carry):
        st, ot = carry
        rows = pl.ds(pl.multiple_of(grp * SUBLANES, SUBLANES), SUBLANES)
        lg, k, q = lg_ref[rows, :], k_ref[rows, :], q_ref[rows, :]
        for i in range(SUBLANES):
            onehot = (lane == grp * SUBLANES + i).astype(F32)
            vcol = jnp.sum(vt * onehot, axis=1, keepdims=True)
            st = st * jnp.exp(lg[i:i + 1, :]) + vcol * k[i:i + 1, :]
            ocol = jnp.sum(st * q[i:i + 1, :], axis=1, keepdims=True)
            ot = ot + ocol * onehot
        return st, ot

    st, ot = lax.fori_loop(0, CHUNK // SUBLANES, token_group, (st, jnp.zeros((HEAD_DIM, CHUNK), F32)))
    return ot.T, st


def _cast_slabs(src_refs, dst_refs):
    for src, dst in zip(src_refs, dst_refs):
        dst[...] = src[...].astype(BF16)


def _slab_specs(weights, n_steps):
    ins, outs, shapes = [], [], []
    for w, layer in weights:
        _, rows, cols = w.shape
        slab = rows // n_steps
        assert slab * n_steps == rows and slab % 16 == 0
        ins.append(pl.BlockSpec((None, slab, cols), lambda s, layer=layer: (layer, jnp.minimum(s, n_steps - 1), 0)))
        outs.append(pl.BlockSpec((slab, cols), lambda s: (jnp.minimum(s, n_steps - 1), 0)))
        shapes.append(jax.ShapeDtypeStruct((rows, cols), BF16))
    return ins, outs, shapes


def _hgrn_mix_kernel(x_ref, gain_ref, lb_ref, w_ref, onorm_ref, wa_ref, wb_ref, wc_ref,
                     gated_ref, s_out_ref, wa_out, wb_out, wc_out,
                     q_s, k_s, lg_s, v_s, og_s, st_ref, slow_ref, *, tm, tiles_per_seq, n_tiles):
    s = pl.program_id(0)
    _cast_slabs((wa_ref, wb_ref, wc_ref), (wa_out, wb_out, wc_out))
    cur, prev = s % 2, (s + 1) % 2
    t_prev = jnp.maximum(s - 1, 0) % tiles_per_seq
    n_chunks = tm // CHUNK
    assert n_chunks == 4

    @pl.when(s == 0)
    def _():
        for ref in (q_s, k_s, lg_s, v_s, og_s):
            ref[1] = jnp.zeros(ref.shape[1:], ref.dtype)
        slow_ref[0] = 0

    @pl.when(t_prev == 0)
    def _():
        st_ref[...] = jnp.zeros_like(st_ref)

    def project(j, r, u, lb):
        rs = slice(r * (tm // 2), (r + 1) * (tm // 2))
        val = _inproj_group(j, u[rs, :], w_ref, lb)
        if j == 0:
            q_s[cur, rs, :] = val
        elif j == 1:
            k_s[cur, rs, :] = 1.0 - val
            lg_s[cur, rs, :] = jnp.log(val)
        elif j == 2:
            v_s[cur, rs, :] = val.astype(BF16)
        else:
            og_s[cur, rs, :] = val

    def recur(rows, heads, big_g):
        for h in heads:
            cols = slice(h * HEAD_DIM, (h + 1) * HEAD_DIM)
            v = v_s[prev, rows, cols]
            if big_g is None:
                o, st = _rec_chunk_stepwise(q_s.at[prev, rows, cols], k_s.at[prev, rows, cols],
                                            lg_s.at[prev, rows, cols], v, st_ref[h])
            else:
                o, st = _rec_chunk_factored(q_s[prev, rows, cols], k_s[prev, rows, cols], big_g[:, cols], v, st_ref[h])
            st_ref[h] = st
            gated_ref[rows, cols] = _head_gate(o, onorm_ref[:, cols], og_s[prev, rows, cols])

    def body(stepwise):
        u = _rms(x_ref[...], gain_ref[...]).astype(BF16)
        lb = _layer_lower_bound(lb_ref)
        for j in range(4):
            rows = slice(j * CHUNK, (j + 1) * CHUNK)
            big_g = None if stepwise else _cumsum_rows(lg_s[prev, rows, :])
            for r in range(2):
                project(j, r, u, lb)
                if not stepwise:
                    recur(rows, range(r * HEADS // 2, (r + 1) * HEADS // 2), big_g)
        if stepwise:

            def chunk(c, carry):
                recur(pl.ds(pl.multiple_of(c * CHUNK, CHUNK), CHUNK), range(HEADS), None)
                return carry

            lax.fori_loop(0, n_chunks, chunk, 0)
        worst = None
        for c in range(n_chunks):
            tot = jnp.sum(lg_s[cur, c * CHUNK:(c + 1) * CHUNK, :], axis=0, keepdims=True)
            worst = tot if worst is None else jnp.minimum(worst, tot)
        slow_ref[0] = (jnp.min(worst) < -LOG_DECAY_LIMIT).astype(jnp.int32)

        @pl.when(t_prev == tiles_per_seq - 1)
        def _():
            for h in range(HEADS):
                s_out_ref[0, h] = st_ref[h].T

    lax.cond(slow_ref[0] == 0, lambda: body(False), lambda: body(True))


def _hgrn_mix(x, gain, lb, w_in, onorm, later_weights, seqs, tm):
    m = x.shape[0]
    n_tiles = m // tm
    w_ins, w_outs, w_shapes = _slab_specs(later_weights, n_tiles)
    tps = n_tiles // seqs
    cur = lambda s: jnp.minimum(s, n_tiles - 1)
    prev = lambda s: jnp.maximum(s - 1, 0)
    vec = _const_spec((1, D_MODEL))
    buf = lambda dt: pltpu.VMEM((2, tm, D_MODEL), dt)
    return pl.pallas_call(
        functools.partial(_hgrn_mix_kernel, tm=tm, tiles_per_seq=tps, n_tiles=n_tiles),
        grid=(n_tiles + 1,),
        in_specs=[pl.BlockSpec((tm, D_MODEL), lambda s: (cur(s), 0)), vec, _const_spec(lb.shape),
                  _const_spec(w_in.shape), vec] + w_ins,
        out_specs=[pl.BlockSpec((tm, D_MODEL), lambda s: (prev(s), 0)),
                   pl.BlockSpec((1, HEADS, HEAD_DIM, HEAD_DIM), lambda s: (prev(s) // tps, 0, 0, 0))] + w_outs,
        out_shape=[jax.ShapeDtypeStruct((m, D_MODEL), BF16),
                   jax.ShapeDtypeStruct((seqs, HEADS, HEAD_DIM, HEAD_DIM), F32)] + w_shapes,
        scratch_shapes=[buf(F32), buf(F32), buf(F32), buf(BF16), buf(F32),
                        pltpu.VMEM((HEADS, HEAD_DIM, HEAD_DIM), F32), pltpu.SMEM((1,), jnp.int32)],
        compiler_params=pltpu.CompilerParams(dimension_semantics=("arbitrary",), vmem_limit_bytes=VMEM_LIMIT),
        name="hgrn_mix_prompt",
    )(x, gain, lb, w_in, onorm, *[w for w, _ in later_weights])


DEC_PER_STEP = 4


def _decode_state_update(step, j, qt_ref, ft_ref, v_ref, s_ref, s_out_ref, o_all_ref):
    shift = (HEAD_DIM - (DEC_PER_STEP * step + j)) % HEAD_DIM
    pieces = []
    for h in range(HEADS):
        cols = slice(h * HEAD_DIM, (h + 1) * HEAD_DIM)
        fcol = pltpu.roll(ft_ref[h], shift, axis=1)[:, 0:1]
        qcol = pltpu.roll(qt_ref[h], shift, axis=1)[:, 0:1]
        vrow = v_ref[j:j + 1, cols]
        s_new = vrow + fcol * (s_ref[j, h] - vrow)
        s_out_ref[j, h] = s_new
        pieces.append(jnp.sum(qcol * s_new, axis=0, keepdims=True))
    assert SUBLANES == 2 * DEC_PER_STEP
    group = pl.ds(pl.multiple_of((step // 2) * SUBLANES, SUBLANES), SUBLANES)
    mine = lax.broadcasted_iota(jnp.int32, (SUBLANES, 1), 0) == (step % 2) * DEC_PER_STEP + j
    o_all_ref[group, :] = jnp.where(mine, jnp.concatenate(pieces, axis=-1), o_all_ref[group, :])


def _hgrn_out_mlp_kernel(gated_ref, x_ref, wout_ref, gain_ref, wup_ref, wdn_ref, *rest, n_cast):
    cast_in, side_in, (og_ref, onorm_ref, xdec_ref) = rest[:n_cast], rest[n_cast:n_cast + 4], rest[n_cast + 4:n_cast + 7]
    out_ref = rest[n_cast + 7]
    cast_out = rest[n_cast + 8:2 * n_cast + 8]
    s_out_ref, dec_out_ref, o_all_ref = rest[2 * n_cast + 8:]
    i = pl.program_id(0)
    assert D_FF // FF_CHUNK == DEC_PER_STEP

    def mlp(x1, side_work):
        hn = _rms(x1, gain_ref[...]).astype(BF16)
        acc = x1
        for c in range(D_FF // FF_CHUNK):
            cols = slice(c * FF_CHUNK, (c + 1) * FF_CHUNK)
            a = jnp.square(jnp.maximum(_dot(hn, wup_ref[:, cols]), 0.0)).astype(BF16)
            acc = acc + _dot(a, wdn_ref[cols, :])
            side_work(c)
        return acc

    @pl.when(i == 0)
    def _():
        o_all_ref[...] = jnp.zeros_like(o_all_ref)

    def prompt_tile():
        _cast_slabs(cast_in, cast_out)
        x1 = x_ref[...] + _dot(gated_ref[...], wout_ref[...])
        out_ref[...] = mlp(x1, lambda c: _decode_state_update(i, c, *side_in, s_out_ref, o_all_ref))

    def decode_tokens():
        gated = jnp.concatenate(
            [_head_gate(o_all_ref[:, h * HEAD_DIM:(h + 1) * HEAD_DIM], onorm_ref[:, h * HEAD_DIM:(h + 1) * HEAD_DIM],
                        og_ref[:, h * HEAD_DIM:(h + 1) * HEAD_DIM]) for h in range(HEADS)], axis=-1)
        dec_out_ref[...] = mlp(xdec_ref[...] + _dot(gated, wout_ref[...]), lambda c: None)

    lax.cond(i < pl.num_programs(0) - 1, prompt_tile, decode_tokens)


def _hgrn_out_mlp(gated, x, w_out, gain, w_up, w_dn, tm, later_weights, decode):
    m = x.shape[0]
    steps = m // tm
    qt, ft, v, s0, og, onorm, xdec = decode
    n = xdec.shape[0]
    assert v.shape == (steps, DEC_PER_STEP, D_MODEL) and s0.shape[0] == n == steps * DEC_PER_STEP
    last = lambda i: jnp.minimum(i, steps - 1)
    row = pl.BlockSpec((tm, D_MODEL), lambda i: (last(i), 0))
    w_ins, w_outs, w_shapes = _slab_specs(later_weights, steps)
    tok = pl.BlockSpec((None, DEC_PER_STEP, D_MODEL), lambda i: (last(i), 0, 0))
    state = pl.BlockSpec((DEC_PER_STEP, HEADS, HEAD_DIM, HEAD_DIM), lambda i: (last(i), 0, 0, 0))
    dec = pl.BlockSpec((n, D_MODEL), lambda i: (0, 0))
    return pl.pallas_call(
        functools.partial(_hgrn_out_mlp_kernel, n_cast=len(later_weights)),
        grid=(steps + 1,),
        in_specs=[row, row, _const_spec(w_out.shape), _const_spec((1, D_MODEL)), _const_spec(w_up.shape),
                  _const_spec(w_dn.shape)] + w_ins
                 + [_const_spec(qt.shape), _const_spec(ft.shape), tok, state,
                    _const_spec(og.shape), _const_spec(onorm.shape), _const_spec(xdec.shape)],
        out_specs=[row] + w_outs + [state, dec],
        out_shape=[jax.ShapeDtypeStruct((m, D_MODEL), F32)] + w_shapes
                  + [jax.ShapeDtypeStruct(s0.shape, F32), jax.ShapeDtypeStruct((n, D_MODEL), F32)],
        scratch_shapes=[pltpu.VMEM((n, D_MODEL), F32)],
        compiler_params=pltpu.CompilerParams(dimension_semantics=("arbitrary",), vmem_limit_bytes=VMEM_LIMIT),
        name="hgrn_out_mlp",
    )(gated, x, w_out, gain, w_up, w_dn, *[w for w, _ in later_weights], qt, ft, v, s0, og, onorm, xdec)


def _pool_project(z, wgrp_ref, scale_ref):
    parts = [_dot(z[:, g * POOL_GW:(g + 1) * POOL_GW].astype(BF16), wgrp_ref[g]) for g in range(len(POOL_WINDOWS))]
    return jnp.concatenate(parts, axis=-1) * scale_ref[...]


def _shift_rows(a, k):
    return pltpu.roll(a, k, axis=0)


def _pool_prompt_kernel(x_ref, halo_ref, pgain_ref, wgrp_ref, scale_ref, gain_ref, wup_ref, wdn_ref, fgain_ref,
                        xd_ref, xd_all_ref, dhist_ref, y_ref, hist_ref, yd_ref, dhist_out_ref,
                        x1_ref, hn_ref, pre_ref, zd_ref, *, tm, tiles_per_seq, n_tiles, dec_blocks):
    s = pl.program_id(0)
    t = jnp.minimum(s, n_tiles - 1) % tiles_per_seq
    cur, prev = s % 2, (s + 1) % 2
    assert D_FF // FF_CHUNK == len(POOL_WINDOWS) == 4

    def finish():
        y_ref[0] = _rms(pre_ref[...], fgain_ref[...])

    def decode_mixer():
        blk = jnp.minimum(s, dec_blocks - 1)
        u = _rms(xd_ref[...], pgain_ref[...])
        parts = []
        for g, w in enumerate(POOL_WINDOWS):
            cols = slice(g * POOL_GW, (g + 1) * POOL_GW)
            acc = u[:, cols]
            for j in range(1, w):
                acc = acc + dhist_ref[POOL_HIST - j, :, cols]
            parts.append(acc / float(min(w, PAST_LEN + 1)) - u[:, cols])
        zd_ref[pl.ds(pl.multiple_of(blk * DEC_BLOCK, DEC_BLOCK), DEC_BLOCK), :] = jnp.concatenate(parts, axis=-1)
        dhist_out_ref[0:POOL_HIST - 1] = dhist_ref[1:POOL_HIST]
        dhist_out_ref[POOL_HIST - 1] = u

    def decode_finish():
        xd = xd_all_ref[...]
        x1 = xd + _pool_project(zd_ref[...], wgrp_ref, scale_ref)
        hn = _rms(x1, gain_ref[...]).astype(BF16)
        acc = x1
        for c in range(D_FF // FF_CHUNK):
            ff = slice(c * FF_CHUNK, (c + 1) * FF_CHUNK)
            a = jnp.square(jnp.maximum(_dot(hn, wup_ref[:, ff]), 0.0)).astype(BF16)
            acc = acc + _dot(a, wdn_ref[ff, :])
        yd_ref[...] = _rms(acc, fgain_ref[...])

    def mixer_and_mlp(with_mlp):
        x = x_ref[0]
        u = _rms(x, pgain_ref[...])
        halo = jnp.where(t > 0, _rms(halo_ref[0], pgain_ref[...]), 0.0)
        ext = jnp.concatenate([halo, u], axis=0)
        pos = t * tm + lax.broadcasted_iota(jnp.int32, (tm, 1), 0)

        def pool_group(g):
            w = POOL_WINDOWS[g]
            cols = slice(g * POOL_GW, (g + 1) * POOL_GW)
            win = ext[:, cols]
            k = 1
            while k < w:
                win = win + _shift_rows(win, k)
                k *= 2
            cnt = jnp.minimum(w, pos + 1).astype(F32)
            z = (win[HALO:, :] / cnt - u[:, cols]).astype(BF16)
            return x[:, cols] + _dot(z, wgrp_ref[g]) * scale_ref[:, cols]

        def mlp_chunk(c, hn, acc):
            ff = slice(c * FF_CHUNK, (c + 1) * FF_CHUNK)
            a = jnp.square(jnp.maximum(_dot(hn, wup_ref[:, ff]), 0.0)).astype(BF16)
            return acc + _dot(a, wdn_ref[ff, :])

        if with_mlp:
            hn_prev = hn_ref[prev]
            acc = mlp_chunk(0, hn_prev, x1_ref[prev])
        parts = [pool_group(0), pool_group(1)]
        if with_mlp:
            acc = mlp_chunk(1, hn_prev, acc)
        parts += [pool_group(2), pool_group(3)]
        if with_mlp:
            acc = mlp_chunk(2, hn_prev, acc)
        decode_mixer()
        x1_new = jnp.concatenate(parts, axis=-1)
        x1_ref[cur] = x1_new
        hn_ref[cur] = _rms(x1_new, gain_ref[...]).astype(BF16)
        if with_mlp:
            pre_ref[...] = mlp_chunk(3, hn_prev, acc)

        @pl.when(t == tiles_per_seq - 1)
        def _():
            hist_ref[0] = ext[HALO + tm - POOL_HIST:, :]

    def head():
        pre_ref[...] = jnp.zeros_like(pre_ref)
        mixer_and_mlp(False)

    def full():
        finish()
        mixer_and_mlp(True)

    def tail():
        finish()
        decode_finish()

    lax.cond(s == 0, head, lambda: lax.cond(s == n_tiles + 1, tail, full))


DEC_BLOCK = 8


def _pool_prompt(x, pgain, wgrp, scale, gain, w_up, w_dn, fgain, xdec, dhist, tm):
    b, l, _ = x.shape
    n = xdec.shape[0]
    vec = _const_spec((1, D_MODEL))
    tps = l // tm
    n_tiles = b * tps
    dec_blocks = n // DEC_BLOCK
    assert dec_blocks * DEC_BLOCK == n and dec_blocks <= n_tiles + 1
    blocks_per_tile = tm // HALO
    cur = lambda s: jnp.minimum(s, n_tiles - 1)
    done = lambda s: jnp.maximum(s - 2, 0)
    dblk = lambda s: jnp.minimum(s, dec_blocks - 1)
    dec_hist = pl.BlockSpec((POOL_HIST, DEC_BLOCK, D_MODEL), lambda s: (0, dblk(s), 0))
    return pl.pallas_call(
        functools.partial(_pool_prompt_kernel, tm=tm, tiles_per_seq=tps, n_tiles=n_tiles, dec_blocks=dec_blocks),
        grid=(n_tiles + 2,),
        in_specs=[pl.BlockSpec((1, tm, D_MODEL), lambda s: (cur(s) // tps, cur(s) % tps, 0)),
                  pl.BlockSpec((1, HALO, D_MODEL),
                               lambda s: (cur(s) // tps, jnp.maximum((cur(s) % tps) * blocks_per_tile - 1, 0), 0)),
                  vec, _const_spec(wgrp.shape), vec, vec, _const_spec(w_up.shape), _const_spec(w_dn.shape), vec,
                  pl.BlockSpec((DEC_BLOCK, D_MODEL), lambda s: (dblk(s), 0)), _const_spec(xdec.shape), dec_hist],
        out_specs=[pl.BlockSpec((1, tm, D_MODEL), lambda s: (done(s) // tps, done(s) % tps, 0)),
                   pl.BlockSpec((1, POOL_HIST, D_MODEL), lambda s: (cur(s) // tps, 0, 0)),
                   pl.BlockSpec(xdec.shape, lambda s: (0, 0)), dec_hist],
        out_shape=[jax.ShapeDtypeStruct((b, l, D_MODEL), F32), jax.ShapeDtypeStruct((b, POOL_HIST, D_MODEL), F32),
                   jax.ShapeDtypeStruct(xdec.shape, F32), jax.ShapeDtypeStruct(dhist.shape, F32)],
        scratch_shapes=[pltpu.VMEM((2, tm, D_MODEL), F32), pltpu.VMEM((2, tm, D_MODEL), BF16),
                        pltpu.VMEM((tm, D_MODEL), F32), pltpu.VMEM(xdec.shape, F32)],
        compiler_params=pltpu.CompilerParams(dimension_semantics=("arbitrary",), vmem_limit_bytes=VMEM_LIMIT),
        name="pool_mlp_prompt",
    )(x, x, pgain, wgrp, scale, gain, w_up, w_dn, fgain, xdec, xdec, dhist)


def kernel(x_prompt, x_sample, state_hgrn, state_pool, hgrn_norm, hgrn_w_in, hgrn_lb, hgrn_onorm, hgrn_w_out,
           pool_norm, pool_w, pool_scale, mlp_norm, mlp_up, mlp_down, final_norm):
    b, l, _ = x_prompt.shape
    n = x_sample.shape[0]
    vec = lambda a: a.reshape(1, D_MODEL)
    onorm = vec(hgrn_onorm[0])
    xp = x_prompt.reshape(b * l, D_MODEL)
    xs = x_sample.reshape(n, D_MODEL)
    tm = PROMPT_TILE
    steps = b * l // tm

    w_in, qt, ft, v, og = _inproj_decode(xs, vec(hgrn_norm[0]), hgrn_lb, hgrn_w_in)

    gated, hgrn_p, w_up0, w_dn0, w_out = _hgrn_mix(xp, vec(hgrn_norm[0]), hgrn_lb, w_in, onorm,
                                                   ((mlp_up, 0), (mlp_down, 0), (hgrn_w_out, 0)), b, tm)
    pool_w_rows = pool_w.reshape(pool_w.shape[0], D_MODEL, POOL_GW)
    x2, w_up1, w_dn1, w_pool, hgrn_s, x2_dec = _hgrn_out_mlp(
        gated, xp, w_out, vec(mlp_norm[0]), w_up0, w_dn0, tm, ((mlp_up, 1), (mlp_down, 1), (pool_w_rows, 0)),
        (qt, ft, v.reshape(steps, n // steps, D_MODEL), state_hgrn[0], og, onorm, xs))
    w_pool = w_pool.reshape(len(POOL_WINDOWS), POOL_GW, POOL_GW)
    pool_args = (vec(pool_norm[0]), w_pool, vec(pool_scale[0]), vec(mlp_norm[1]), w_up1, w_dn1, vec(final_norm))
    y_prompt, pool_p, y_sample, pool_s = _pool_prompt(x2.reshape(b, l, D_MODEL), *pool_args, x2_dec,
                                                      jnp.transpose(state_pool[0], (1, 0, 2)), tm=tm)

    return (y_prompt, y_sample.reshape(n, 1, D_MODEL), hgrn_p[None], hgrn_s[None], pool_p[None],
            jnp.transpose(pool_s, (1, 0, 2))[None])
```

```python
import functools

import jax
import jax.numpy as jnp
from jax import lax
from jax.experimental import pallas as pl
from jax.experimental.pallas import tpu as pltpu

D_MODEL = 1024
HEADS = 8
HEAD_DIM = 128
SUBLANES = 8
D_FF = 4 * D_MODEL
POOL_WINDOWS = (2, 4, 8, 16)
POOL_GW = D_MODEL // len(POOL_WINDOWS)
POOL_HIST = max(POOL_WINDOWS) - 1
HALO = 16
EPS = 1e-6
PAST_LEN = 16384

PROMPT_TILE = 512
CHUNK = 128
LOG_DECAY_LIMIT = 80.0
FF_CHUNK = 1024
VMEM_LIMIT = 56 * 1024 * 1024

BF16 = jnp.bfloat16
F32 = jnp.float32


def _const_spec(shape):
    nd = len(shape)
    return pl.BlockSpec(shape, lambda *_: (0,) * nd, pipeline_mode=pl.Buffered(1))


def _rms(x, gain):
    return x * lax.rsqrt(jnp.mean(x * x, axis=-1, keepdims=True) + EPS) * gain


def _sigmoid(x):
    return 1.0 / (1.0 + jnp.exp(-x))


def _dot(a, b):
    return jnp.dot(a, b, preferred_element_type=F32)


def _layer_lower_bound(lb_ref):
    lbr = lb_ref[...]
    e = jnp.exp(lbr - jnp.max(lbr, axis=0, keepdims=True))
    return e[0:1, :] / jnp.sum(e, axis=0, keepdims=True)


def _inproj_activation(j, p, lb):
    if j == 0:
        return p * _sigmoid(p) * (HEAD_DIM ** -0.5)
    if j == 1:
        return lb + (1.0 - lb) * _sigmoid(p)
    if j == 2:
        return p
    return _sigmoid(p)


def _inproj_group(j, u, w_ref, lb):
    return _inproj_activation(j, _dot(u, w_ref[:, j * D_MODEL:(j + 1) * D_MODEL]), lb)


def _inproj_decode_kernel(x_ref, gain_ref, lb_ref, w_ref, wb_ref, qt_ref, ft_ref, v_ref, og_ref):
    j = pl.program_id(0)
    wb = w_ref[...].astype(BF16)
    wb_ref[...] = wb
    p = _dot(_rms(x_ref[:, 0, :], gain_ref[...]).astype(BF16), wb)
    lb = _layer_lower_bound(lb_ref)

    def heads_transposed(val, out_ref):
        for h in range(HEADS):
            out_ref[h] = val[:, h * HEAD_DIM:(h + 1) * HEAD_DIM].T

    @pl.when(j == 0)
    def _():
        heads_transposed(_inproj_activation(0, p, lb), qt_ref)

    @pl.when(j == 1)
    def _():
        heads_transposed(_inproj_activation(1, p, lb), ft_ref)

    @pl.when(j == 2)
    def _():
        v_ref[...] = p

    @pl.when(j == 3)
    def _():
        og_ref[...] = _inproj_activation(3, p, lb)


def _inproj_decode(x, gain, lb, w_in_f32):
    n = x.shape[0]
    vec = _const_spec((1, D_MODEL))
    tok = pl.BlockSpec((n, D_MODEL), lambda j: (0, 0))
    per_head = pl.BlockSpec((HEADS, HEAD_DIM, n), lambda j: (0, 0, 0))
    return pl.pallas_call(
        _inproj_decode_kernel,
        grid=(4,),
        in_specs=[_const_spec(x.shape), vec, _const_spec(lb.shape),
                  pl.BlockSpec((None, D_MODEL, D_MODEL), lambda j: (0, 0, j))],
        out_specs=[pl.BlockSpec((D_MODEL, D_MODEL), lambda j: (0, j)), per_head, per_head, tok, tok],
        out_shape=[jax.ShapeDtypeStruct((D_MODEL, 4 * D_MODEL), BF16),
                   jax.ShapeDtypeStruct((HEADS, HEAD_DIM, n), F32), jax.ShapeDtypeStruct((HEADS, HEAD_DIM, n), F32),
                   jax.ShapeDtypeStruct((n, D_MODEL), F32), jax.ShapeDtypeStruct((n, D_MODEL), F32)],
        compiler_params=pltpu.CompilerParams(dimension_semantics=("arbitrary",), vmem_limit_bytes=VMEM_LIMIT),
        name="hgrn_inproj_decode",
    )(x, gain, lb, w_in_f32)


def _head_gate(o, onorm, og):
    return (o * lax.rsqrt(jnp.mean(o * o, axis=-1, keepdims=True) + EPS) * onorm * og).astype(BF16)


def _lower_triangle():
    return (lax.broadcasted_iota(jnp.int32, (CHUNK, CHUNK), 0)
            >= lax.broadcasted_iota(jnp.int32, (CHUNK, CHUNK), 1))


def _cumsum_rows(g):
    n = g.shape[0]
    row = lax.broadcasted_iota(jnp.int32, g.shape, 0)
    s = 1
    while s < n:
        g = g + jnp.where(row >= s, pltpu.roll(g, s, axis=0), 0.0)
        s *= 2
    return g


def _rec_chunk_factored(q, k, big_g, v, st):
    tri = _lower_triangle()
    decay = jnp.exp(big_g)
    qd = (q * decay).astype(BF16)
    kd = (k * jnp.exp(-big_g)).astype(BF16)
    a = lax.dot_general(qd, kd, (((1,), (1,)), ((), ())), preferred_element_type=F32)
    a = jnp.where(tri, a, 0.0).astype(BF16)
    o = _dot(a, v) + lax.dot_general(qd, st.astype(BF16), (((1,), (1,)), ((), ())), preferred_element_type=F32)
    ds = lax.dot_general(v, kd, (((0,), (0,)), ((), ())), preferred_element_type=F32)
    return o, (st + ds) * decay[CHUNK - 1:CHUNK, :]


def _rec_chunk_stepwise(q_ref, k_ref, lg_ref, v, st):
    vt = v.astype(F32).T
    lane = lax.broadcasted_iota(jnp.int32, (1, CHUNK), 1)

    def token_group(grp, carry):
        st, ot = carry
        rows = pl.ds(pl.multiple_of(grp * SUBLANES, SUBLANES), SUBLANES)
        lg, k, q = lg_ref[rows, :], k_ref[rows, :], q_ref[rows, :]
        for i in range(SUBLANES):
            onehot = (lane == grp * SUBLANES + i).astype(F32)
            vcol = jnp.sum(vt * onehot, axis=1, keepdims=True)
            st = st * jnp.exp(lg[i:i + 1, :]) + vcol * k[i:i + 1, :]
            ocol = jnp.sum(st * q[i:i + 1, :], axis=1, keepdims=True)
            ot = ot + ocol * onehot
        return st, ot

    st, ot = lax.fori_loop(0, CHUNK // SUBLANES, token_group, (st, jnp.zeros((HEAD_DIM, CHUNK), F32)))
    return ot.T, st


def _cast_slabs(src_refs, dst_refs):
    for src, dst in zip(src_refs, dst_refs):
        dst[...] = src[...].astype(BF16)


def _slab_specs(weights, n_steps):
    ins, outs, shapes = [], [], []
    for w, layer in weights:
        _, rows, cols = w.shape
        slab = rows // n_steps
        assert slab * n_steps == rows and slab % 16 == 0
        ins.append(pl.BlockSpec((None, slab, cols), lambda s, layer=layer: (layer, jnp.minimum(s, n_steps - 1), 0)))
        outs.append(pl.BlockSpec((slab, cols), lambda s: (jnp.minimum(s, n_steps - 1), 0)))
        shapes.append(jax.ShapeDtypeStruct((rows, cols), BF16))
    return ins, outs, shapes


def _hgrn_mix_kernel(x_ref, gain_ref, lb_ref, w_ref, onorm_ref, wa_ref, wb_ref, wc_ref,
                     gated_ref, s_out_ref, wa_out, wb_out, wc_out,
                     q_s, k_s, lg_s, v_s, og_s, st_ref, slow_ref, *, tm, tiles_per_seq, n_tiles):
    s = pl.program_id(0)
    _cast_slabs((wa_ref, wb_ref, wc_ref), (wa_out, wb_out, wc_out))
    cur, prev = s % 2, (s + 1) % 2
    t_prev = jnp.maximum(s - 1, 0) % tiles_per_seq
    n_chunks = tm // CHUNK
    assert n_chunks == 4

    @pl.when(s == 0)
    def _():
        for ref in (q_s, k_s, lg_s, v_s, og_s):
            ref[1] = jnp.zeros(ref.shape[1:], ref.dtype)
        slow_ref[0] = 0

    @pl.when(t_prev == 0)
    def _():
        st_ref[...] = jnp.zeros_like(st_ref)

    def project(j, r, u, lb):
        rs = slice(r * (tm // 2), (r + 1) * (tm // 2))
        val = _inproj_group(j, u[rs, :], w_ref, lb)
        if j == 0:
            q_s[cur, rs, :] = val
        elif j == 1:
            k_s[cur, rs, :] = 1.0 - val
            lg_s[cur, rs, :] = jnp.log(val)
        elif j == 2:
            v_s[cur, rs, :] = val.astype(BF16)
        else:
            og_s[cur, rs, :] = val

    def recur(rows, heads, big_g):
        for h in heads:
            cols = slice(h * HEAD_DIM, (h + 1) * HEAD_DIM)
            v = v_s[prev, rows, cols]
            if big_g is None:
                o, st = _rec_chunk_stepwise(q_s.at[prev, rows, cols], k_s.at[prev, rows, cols],
                                            lg_s.at[prev, rows, cols], v, st_ref[h])
            else:
                o, st = _rec_chunk_factored(q_s[prev, rows, cols], k_s[prev, rows, cols], big_g[:, cols], v, st_ref[h])
            st_ref[h] = st
            gated_ref[rows, cols] = _head_gate(o, onorm_ref[:, cols], og_s[prev, rows, cols])

    def body(stepwise):
        u = _rms(x_ref[...], gain_ref[...]).astype(BF16)
        lb = _layer_lower_bound(lb_ref)
        for j in range(4):
            rows = slice(j * CHUNK, (j + 1) * CHUNK)
            big_g = None if stepwise else _cumsum_rows(lg_s[prev, rows, :])
            for r in range(2):
                project(j, r, u, lb)
                if not stepwise:
                    recur(rows, range(r * HEADS // 2, (r + 1) * HEADS // 2), big_g)
        if stepwise:

            def chunk(c, carry):
                recur(pl.ds(pl.multiple_of(c * CHUNK, CHUNK), CHUNK), range(HEADS), None)
                return carry

            lax.fori_loop(0, n_chunks, chunk, 0)
        worst = None
        for c in range(n_chunks):
            tot = jnp.sum(lg_s[cur, c * CHUNK:(c + 1) * CHUNK, :], axis=0, keepdims=True)
            worst = tot if worst is None else jnp.minimum(worst, tot)
        slow_ref[0] = (jnp.min(worst) < -LOG_DECAY_LIMIT).astype(jnp.int32)

        @pl.when(t_prev == tiles_per_seq - 1)
        def _():
            for h in range(HEADS):
                s_out_ref[0, h] = st_ref[h].T

    def last_recurrence():
        for j in range(n_chunks):
            rows = slice(j * CHUNK, (j + 1) * CHUNK)
            recur(rows, range(HEADS), _cumsum_rows(lg_s[prev, rows, :]))

        @pl.when(t_prev == tiles_per_seq - 1)
        def _():
            for h in range(HEADS):
                s_out_ref[0, h] = st_ref[h].T

    lax.cond(slow_ref[0] == 0,
             lambda: lax.cond(s == n_tiles, last_recurrence, lambda: body(False)), lambda: body(True))


def _hgrn_mix(x, gain, lb, w_in, onorm, later_weights, seqs, tm):
    m = x.shape[0]
    n_tiles = m // tm
    w_ins, w_outs, w_shapes = _slab_specs(later_weights, n_tiles)
    tps = n_tiles // seqs
    cur = lambda s: jnp.minimum(s, n_tiles - 1)
    prev = lambda s: jnp.maximum(s - 1, 0)
    vec = _const_spec((1, D_MODEL))
    buf = lambda dt: pltpu.VMEM((2, tm, D_MODEL), dt)
    return pl.pallas_call(
        functools.partial(_hgrn_mix_kernel, tm=tm, tiles_per_seq=tps, n_tiles=n_tiles),
        grid=(n_tiles + 1,),
        in_specs=[pl.BlockSpec((tm, D_MODEL), lambda s: (cur(s), 0)), vec, _const_spec(lb.shape),
                  _const_spec(w_in.shape), vec] + w_ins,
        out_specs=[pl.BlockSpec((tm, D_MODEL), lambda s: (prev(s), 0)),
                   pl.BlockSpec((1, HEADS, HEAD_DIM, HEAD_DIM), lambda s: (prev(s) // tps, 0, 0, 0))] + w_outs,
        out_shape=[jax.ShapeDtypeStruct((m, D_MODEL), BF16),
                   jax.ShapeDtypeStruct((seqs, HEADS, HEAD_DIM, HEAD_DIM), F32)] + w_shapes,
        scratch_shapes=[buf(F32), buf(F32), buf(F32), buf(BF16), buf(F32),
                        pltpu.VMEM((HEADS, HEAD_DIM, HEAD_DIM), F32), pltpu.SMEM((1,), jnp.int32)],
        compiler_params=pltpu.CompilerParams(dimension_semantics=("arbitrary",), vmem_limit_bytes=VMEM_LIMIT),
        name="hgrn_mix_prompt",
    )(x, gain, lb, w_in, onorm, *[w for w, _ in later_weights])


DEC_PER_STEP = 4


def _decode_state_update(step, j, qt_ref, ft_ref, v_ref, s_ref, s_out_ref, o_all_ref):
    shift = (HEAD_DIM - (DEC_PER_STEP * step + j)) % HEAD_DIM
    pieces = []
    for h in range(HEADS):
        cols = slice(h * HEAD_DIM, (h + 1) * HEAD_DIM)
        fcol = pltpu.roll(ft_ref[h], shift, axis=1)[:, 0:1]
        qcol = pltpu.roll(qt_ref[h], shift, axis=1)[:, 0:1]
        vrow = v_ref[j:j + 1, cols]
        s_new = vrow + fcol * (s_ref[j, h] - vrow)
        s_out_ref[j, h] = s_new
        pieces.append(jnp.sum(qcol * s_new, axis=0, keepdims=True))
    assert SUBLANES == 2 * DEC_PER_STEP
    group = pl.ds(pl.multiple_of((step // 2) * SUBLANES, SUBLANES), SUBLANES)
    mine = lax.broadcasted_iota(jnp.int32, (SUBLANES, 1), 0) == (step % 2) * DEC_PER_STEP + j
    o_all_ref[group, :] = jnp.where(mine, jnp.concatenate(pieces, axis=-1), o_all_ref[group, :])


def _hgrn_out_mlp_kernel(gated_ref, x_ref, wout_ref, gain_ref, wup_ref, wdn_ref, *rest, n_cast):
    cast_in, side_in, (og_ref, onorm_ref, xdec_ref) = rest[:n_cast], rest[n_cast:n_cast + 4], rest[n_cast + 4:n_cast + 7]
    out_ref = rest[n_cast + 7]
    cast_out = rest[n_cast + 8:2 * n_cast + 8]
    s_out_ref, dec_out_ref, o_all_ref = rest[2 * n_cast + 8:]
    i = pl.program_id(0)
    assert D_FF // FF_CHUNK == DEC_PER_STEP

    def mlp(x1, side_work):
        hn = _rms(x1, gain_ref[...]).astype(BF16)
        acc = x1
        for c in range(D_FF // FF_CHUNK):
            cols = slice(c * FF_CHUNK, (c + 1) * FF_CHUNK)
            a = jnp.square(jnp.maximum(_dot(hn, wup_ref[:, cols]), 0.0)).astype(BF16)
            acc = acc + _dot(a, wdn_ref[cols, :])
            side_work(c)
        return acc

    @pl.when(i == 0)
    def _():
        o_all_ref[...] = jnp.zeros_like(o_all_ref)

    def prompt_tile():
        _cast_slabs(cast_in, cast_out)
        x1 = x_ref[...] + _dot(gated_ref[...], wout_ref[...])
        out_ref[...] = mlp(x1, lambda c: _decode_state_update(i, c, *side_in, s_out_ref, o_all_ref))

    def decode_tokens():
        gated = jnp.concatenate(
            [_head_gate(o_all_ref[:, h * HEAD_DIM:(h + 1) * HEAD_DIM], onorm_ref[:, h * HEAD_DIM:(h + 1) * HEAD_DIM],
                        og_ref[:, h * HEAD_DIM:(h + 1) * HEAD_DIM]) for h in range(HEADS)], axis=-1)
        dec_out_ref[...] = mlp(xdec_ref[:, 0, :] + _dot(gated, wout_ref[...]), lambda c: None)

    lax.cond(i < pl.num_programs(0) - 1, prompt_tile, decode_tokens)


def _hgrn_out_mlp(gated, x, w_out, gain, w_up, w_dn, tm, later_weights, decode):
    m = x.shape[0]
    steps = m // tm
    qt, ft, v, s0, og, onorm, xdec = decode
    n = xdec.shape[0]
    assert v.shape == (steps, DEC_PER_STEP, D_MODEL) and s0.shape[0] == n == steps * DEC_PER_STEP
    last = lambda i: jnp.minimum(i, steps - 1)
    row = pl.BlockSpec((tm, D_MODEL), lambda i: (last(i), 0))
    w_ins, w_outs, w_shapes = _slab_specs(later_weights, steps)
    tok = pl.BlockSpec((None, DEC_PER_STEP, D_MODEL), lambda i: (last(i), 0, 0))
    state = pl.BlockSpec((DEC_PER_STEP, HEADS, HEAD_DIM, HEAD_DIM), lambda i: (last(i), 0, 0, 0))
    dec = pl.BlockSpec((n, D_MODEL), lambda i: (0, 0))
    return pl.pallas_call(
        functools.partial(_hgrn_out_mlp_kernel, n_cast=len(later_weights)),
        grid=(steps + 1,),
        in_specs=[row, row, _const_spec(w_out.shape), _const_spec((1, D_MODEL)), _const_spec(w_up.shape),
                  _const_spec(w_dn.shape)] + w_ins
                 + [_const_spec(qt.shape), _const_spec(ft.shape), tok, state,
                    _const_spec(og.shape), _const_spec(onorm.shape), _const_spec(xdec.shape)],
        out_specs=[row] + w_outs + [state, dec],
        out_shape=[jax.ShapeDtypeStruct((m, D_MODEL), F32)] + w_shapes
                  + [jax.ShapeDtypeStruct(s0.shape, F32), jax.ShapeDtypeStruct((n, D_MODEL), F32)],
        scratch_shapes=[pltpu.VMEM((n, D_MODEL), F32)],
        compiler_params=pltpu.CompilerParams(dimension_semantics=("arbitrary",), vmem_limit_bytes=VMEM_LIMIT),
        name="hgrn_out_mlp",
    )(gated, x, w_out, gain, w_up, w_dn, *[w for w, _ in later_weights], qt, ft, v, s0, og, onorm, xdec)


def _pool_project(z, wgrp_ref, scale_ref):
    parts = [_dot(z[:, g * POOL_GW:(g + 1) * POOL_GW].astype(BF16), wgrp_ref[g]) for g in range(len(POOL_WINDOWS))]
    return jnp.concatenate(parts, axis=-1) * scale_ref[...]


def _shift_rows(a, k):
    return pltpu.roll(a, k, axis=0)


def _pool_prompt_kernel(x_ref, halo_ref, pgain_ref, wgrp_ref, scale_ref, gain_ref, wup_ref, wdn_ref, fgain_ref,
                        xd_ref, xd_all_ref, dhist_ref, y_ref, hist_ref, yd_ref, dhist_out_ref,
                        x1_ref, hn_ref, pre_ref, zd_ref, *, tm, tiles_per_seq, n_tiles, dec_blocks):
    s = pl.program_id(0)
    t = jnp.minimum(s, n_tiles - 1) % tiles_per_seq
    cur, prev = s % 2, (s + 1) % 2
    assert D_FF // FF_CHUNK == len(POOL_WINDOWS) == 4

    def finish():
        y_ref[0] = _rms(pre_ref[...], fgain_ref[...])

    def decode_mixer():
        blk = jnp.minimum(s, dec_blocks - 1)
        u = _rms(xd_ref[...], pgain_ref[...])
        parts = []
        for g, w in enumerate(POOL_WINDOWS):
            cols = slice(g * POOL_GW, (g + 1) * POOL_GW)
            acc = u[:, cols]
            for j in range(1, w):
                acc = acc + dhist_ref[POOL_HIST - j, :, cols]
            parts.append(acc / float(min(w, PAST_LEN + 1)) - u[:, cols])
        zd_ref[pl.ds(pl.multiple_of(blk * DEC_BLOCK, DEC_BLOCK), DEC_BLOCK), :] = jnp.concatenate(parts, axis=-1)
        dhist_out_ref[0:POOL_HIST - 1] = dhist_ref[1:POOL_HIST]
        dhist_out_ref[POOL_HIST - 1] = u

    def decode_finish():
        xd = xd_all_ref[...]
        x1 = xd + _pool_project(zd_ref[...], wgrp_ref, scale_ref)
        hn = _rms(x1, gain_ref[...]).astype(BF16)
        acc = x1
        for c in range(D_FF // FF_CHUNK):
            ff = slice(c * FF_CHUNK, (c + 1) * FF_CHUNK)
            a = jnp.square(jnp.maximum(_dot(hn, wup_ref[:, ff]), 0.0)).astype(BF16)
            acc = acc + _dot(a, wdn_ref[ff, :])
        yd_ref[:, 0, :] = _rms(acc, fgain_ref[...])

    def mixer_and_mlp(with_mlp):
        x = x_ref[0]
        u = _rms(x, pgain_ref[...])
        halo = jnp.where(t > 0, _rms(halo_ref[0], pgain_ref[...]), 0.0)
        ext = jnp.concatenate([halo, u], axis=0)
        pos = t * tm + lax.broadcasted_iota(jnp.int32, (tm, 1), 0)

        def pool_group(g):
            w = POOL_WINDOWS[g]
            cols = slice(g * POOL_GW, (g + 1) * POOL_GW)
            win = ext[:, cols]
            k = 1
            while k < w:
                win = win + _shift_rows(win, k)
                k *= 2
            cnt = jnp.minimum(w, pos + 1).astype(F32)
            z = (win[HALO:, :] / cnt - u[:, cols]).astype(BF16)
            return x[:, cols] + _dot(z, wgrp_ref[g]) * scale_ref[:, cols]

        def mlp_chunk(c, hn, acc):
            ff = slice(c * FF_CHUNK, (c + 1) * FF_CHUNK)
            a = jnp.square(jnp.maximum(_dot(hn, wup_ref[:, ff]), 0.0)).astype(BF16)
            return acc + _dot(a, wdn_ref[ff, :])

        if with_mlp:
            hn_prev = hn_ref[prev]
            acc = mlp_chunk(0, hn_prev, x1_ref[prev])
        parts = [pool_group(0), pool_group(1)]
        if with_mlp:
            acc = mlp_chunk(1, hn_prev, acc)
        parts += [pool_group(2), pool_group(3)]
        if with_mlp:
            acc = mlp_chunk(2, hn_prev, acc)
        decode_mixer()
        x1_new = jnp.concatenate(parts, axis=-1)
        x1_ref[cur] = x1_new
        hn_ref[cur] = _rms(x1_new, gain_ref[...]).astype(BF16)
        if with_mlp:
            pre_ref[...] = mlp_chunk(3, hn_prev, acc)

        @pl.when(t == tiles_per_seq - 1)
        def _():
            hist_ref[0] = ext[HALO + tm - POOL_HIST:, :]

    def head():
        pre_ref[...] = jnp.zeros_like(pre_ref)
        mixer_and_mlp(False)

    def full():
        finish()
        mixer_and_mlp(True)

    def tail():
        finish()
        decode_finish()

    lax.cond(s == 0, head, lambda: lax.cond(s == n_tiles + 1, tail, full))


DEC_BLOCK = 8


def _pool_prompt(x, pgain, wgrp, scale, gain, w_up, w_dn, fgain, xdec, dhist, tm):
    b, l, _ = x.shape
    n = xdec.shape[0]
    vec = _const_spec((1, D_MODEL))
    tps = l // tm
    n_tiles = b * tps
    dec_blocks = n // DEC_BLOCK
    assert dec_blocks * DEC_BLOCK == n and dec_blocks <= n_tiles + 1
    blocks_per_tile = tm // HALO
    cur = lambda s: jnp.minimum(s, n_tiles - 1)
    done = lambda s: jnp.maximum(s - 2, 0)
    dblk = lambda s: jnp.minimum(s, dec_blocks - 1)
    dec_hist = pl.BlockSpec((POOL_HIST, DEC_BLOCK, D_MODEL), lambda s: (0, dblk(s), 0))
    return pl.pallas_call(
        functools.partial(_pool_prompt_kernel, tm=tm, tiles_per_seq=tps, n_tiles=n_tiles, dec_blocks=dec_blocks),
        grid=(n_tiles + 2,),
        in_specs=[pl.BlockSpec((1, tm, D_MODEL), lambda s: (cur(s) // tps, cur(s) % tps, 0)),
                  pl.BlockSpec((1, HALO, D_MODEL),
                               lambda s: (cur(s) // tps, jnp.maximum((cur(s) % tps) * blocks_per_tile - 1, 0), 0)),
                  vec, _const_spec(wgrp.shape), vec, vec, _const_spec(w_up.shape), _const_spec(w_dn.shape), vec,
                  pl.BlockSpec((DEC_BLOCK, D_MODEL), lambda s: (dblk(s), 0)), _const_spec(xdec.shape), dec_hist],
        out_specs=[pl.BlockSpec((1, tm, D_MODEL), lambda s: (done(s) // tps, done(s) % tps, 0)),
                   pl.BlockSpec((1, POOL_HIST, D_MODEL), lambda s: (cur(s) // tps, 0, 0)),
                   pl.BlockSpec((n, 1, D_MODEL), lambda s: (0, 0, 0)), dec_hist],
        out_shape=[jax.ShapeDtypeStruct((b, l, D_MODEL), F32), jax.ShapeDtypeStruct((b, POOL_HIST, D_MODEL), F32),
                   jax.ShapeDtypeStruct((n, 1, D_MODEL), F32), jax.ShapeDtypeStruct(dhist.shape, F32)],
        scratch_shapes=[pltpu.VMEM((2, tm, D_MODEL), F32), pltpu.VMEM((2, tm, D_MODEL), BF16),
                        pltpu.VMEM((tm, D_MODEL), F32), pltpu.VMEM(xdec.shape, F32)],
        compiler_params=pltpu.CompilerParams(dimension_semantics=("arbitrary",), vmem_limit_bytes=VMEM_LIMIT),
        name="pool_mlp_prompt",
    )(x, x, pgain, wgrp, scale, gain, w_up, w_dn, fgain, xdec, xdec, dhist)


def kernel(x_prompt, x_sample, state_hgrn, state_pool, hgrn_norm, hgrn_w_in, hgrn_lb, hgrn_onorm, hgrn_w_out,
           pool_norm, pool_w, pool_scale, mlp_norm, mlp_up, mlp_down, final_norm):
    b, l, _ = x_prompt.shape
    n = x_sample.shape[0]
    vec = lambda a: a.reshape(1, D_MODEL)
    onorm = vec(hgrn_onorm[0])
    xp = x_prompt.reshape(b * l, D_MODEL)
    tm = PROMPT_TILE
    steps = b * l // tm

    w_in, qt, ft, v, og = _inproj_decode(x_sample, vec(hgrn_norm[0]), hgrn_lb, hgrn_w_in)

    gated, hgrn_p, w_up0, w_dn0, w_out = _hgrn_mix(xp, vec(hgrn_norm[0]), hgrn_lb, w_in, onorm,
                                                   ((mlp_up, 0), (mlp_down, 0), (hgrn_w_out, 0)), b, tm)
    pool_w_rows = pool_w.reshape(pool_w.shape[0], D_MODEL, POOL_GW)
    x2, w_up1, w_dn1, w_pool, hgrn_s, x2_dec = _hgrn_out_mlp(
        gated, xp, w_out, vec(mlp_norm[0]), w_up0, w_dn0, tm, ((mlp_up, 1), (mlp_down, 1), (pool_w_rows, 0)),
        (qt, ft, v.reshape(steps, n // steps, D_MODEL), state_hgrn[0], og, onorm, x_sample))
    w_pool = w_pool.reshape(len(POOL_WINDOWS), POOL_GW, POOL_GW)
    pool_args = (vec(pool_norm[0]), w_pool, vec(pool_scale[0]), vec(mlp_norm[1]), w_up1, w_dn1, vec(final_norm))
    y_prompt, pool_p, y_sample, pool_s = _pool_prompt(x2.reshape(b, l, D_MODEL), *pool_args, x2_dec,
                                                      jnp.transpose(state_pool[0], (1, 0, 2)), tm=tm)

    return (y_prompt, y_sample, hgrn_p[None], hgrn_s[None], pool_p[None],
            jnp.transpose(pool_s, (1, 0, 2))[None])
```
